```python
import jax, jax.numpy as jnp
from jax import lax
import numpy as np

D_MODEL = 2048
BATCH = 4
SEQ = 8192
DEPTH = 1

CHUNK = 64
MIX_WIDTH = D_MODEL
LRU_WIDTH = MIX_WIDTH // 2
LRU_HEADS = 4
LRU_HEAD_DIM = LRU_WIDTH // LRU_HEADS
LRU_CONV = 4
LRU_C = 8.0
SC_WIDTH = MIX_WIDTH - LRU_WIDTH
SC_GROUPS = 8
SC_CONV = 3
IN_COLS = 2 * LRU_WIDTH + 3 * SC_WIDTH
N_GROUPS = 4
EXPERTS_PER_GROUP = 8
N_EXPERTS = N_GROUPS * EXPERTS_PER_GROUP
TOP_K = 2
D_FF_EXPERT = D_MODEL // 2
MOE_BLOCK = 128
EPS = 1e-6

kernel_name = "hymba_rglru_shortconv_hmoe_adaln"


def rms_norm(x, g):
    xf = x.astype(jnp.float32)
    return xf * lax.rsqrt(jnp.mean(xf * xf, axis=-1, keepdims=True) + EPS) * g.astype(jnp.float32)


def group_rms_norm(y, g, n_groups):
    bn, s, ch = y.shape
    yg = y.astype(jnp.float32).reshape(bn, s, n_groups, ch // n_groups)
    yg = yg * lax.rsqrt(jnp.mean(yg * yg, axis=-1, keepdims=True) + EPS)
    return yg.reshape(bn, s, ch) * g.astype(jnp.float32)


def modulate(h, shift, scale):
    return h * (1.0 + scale[:, None, :]) + shift[:, None, :]


def causal_dwconv(x, w):
    width = w.shape[0]
    s = x.shape[1]
    xp = jnp.pad(x, ((0, 0), (width - 1, 0), (0, 0)))
    y = xp[:, 0:s] * w[0]
    for k in range(1, width):
        y = y + xp[:, k:k + s] * w[k]
    return y


def chunked_linear_scan(a, u):
    bn, s, ch = a.shape
    n_chunks = s // CHUNK
    a_c = a.reshape(bn, n_chunks, CHUNK, ch)
    u_c = u.reshape(bn, n_chunks, CHUNK, ch)

    def combine(left, right):
        a_l, u_l = left
        a_r, u_r = right
        return a_l * a_r, a_r * u_l + u_r

    a_cum, h_loc = lax.associative_scan(combine, (a_c, u_c), axis=2)

    def step(h, inp):
        a_end, h_end = inp
        return a_end * h + h_end, h

    a_end = jnp.transpose(a_cum[:, :, -1], (1, 0, 2))
    h_end = jnp.transpose(h_loc[:, :, -1], (1, 0, 2))
    _, h_in = lax.scan(step, jnp.zeros((bn, ch), jnp.float32), (a_end, h_end))
    h_in = jnp.transpose(h_in, (1, 0, 2))
    h = h_loc + a_cum * h_in[:, :, None, :]
    return h.reshape(bn, s, ch)


def rg_lru(x, w_a, b_a, w_x, b_x, lam):
    bn, s, ch = x.shape
    xh = x.reshape(bn, s, LRU_HEADS, LRU_HEAD_DIM)
    r = jax.nn.sigmoid(jnp.einsum('bshi,hij->bshj', xh, w_a).reshape(bn, s, ch) + b_a)
    i = jax.nn.sigmoid(jnp.einsum('bshi,hij->bshj', xh, w_x).reshape(bn, s, ch) + b_x)
    log_a = -LRU_C * r * jax.nn.softplus(-lam.astype(jnp.float32))
    a = jnp.exp(log_a)
    mult = jnp.sqrt(-jnp.expm1(2.0 * log_a))
    return chunked_linear_scan(a, mult * (i * x))


def hier_moe(h, w_rg, b_rg, w_re, b_re, w1, w3, w2):
    bn, s, d = h.shape
    n_tok = bn * s
    t = h.reshape(n_tok, d)
    tf = t.astype(jnp.float32)
    g_prob = jax.nn.softmax(tf @ w_rg + b_rg, axis=-1)
    g_p, g_idx = lax.top_k(g_prob, 1)
    e_logits = (tf @ w_re + b_re).reshape(n_tok, N_GROUPS, EXPERTS_PER_GROUP)
    e_logits = jnp.einsum('nge,ng->ne', e_logits, jax.nn.one_hot(g_idx[:, 0], N_GROUPS, dtype=jnp.float32))
    e_prob = jax.nn.softmax(e_logits, axis=-1)
    e_p, e_idx = lax.top_k(e_prob, TOP_K)
    e_p = e_p / jnp.sum(e_p, axis=-1, keepdims=True)
    gate = g_p * e_p
    expert = g_idx * EXPERTS_PER_GROUP + e_idx

    n_assign = n_tok * TOP_K
    flat_e = expert.reshape(n_assign)
    flat_t = jnp.repeat(jnp.arange(n_tok, dtype=jnp.int32), TOP_K)
    flat_w = gate.reshape(n_assign)
    order = jnp.argsort(flat_e)
    se = flat_e[order]
    counts = jnp.bincount(flat_e, length=N_EXPERTS)
    padded = ((counts + MOE_BLOCK - 1) // MOE_BLOCK) * MOE_BLOCK
    starts = jnp.cumsum(counts) - counts
    pends = jnp.cumsum(padded)
    pstarts = pends - padded
    dest = pstarts[se] + jnp.arange(n_assign, dtype=jnp.int32) - starts[se]
    n_blocks = -(-n_assign // MOE_BLOCK) + N_EXPERTS
    n_slots = n_blocks * MOE_BLOCK
    slot_tok = jnp.zeros((n_slots,), jnp.int32).at[dest].set(flat_t[order])
    slot_w = jnp.zeros((n_slots,), jnp.float32).at[dest].set(flat_w[order])
    block_e = jnp.minimum(
        jnp.searchsorted(pends, jnp.arange(n_blocks, dtype=jnp.int32) * MOE_BLOCK, side='right'),
        N_EXPERTS - 1)
    xb = t[slot_tok].reshape(n_blocks, MOE_BLOCK, d)

    def expert_block(args):
        xblk, e = args
        hid = jax.nn.silu(xblk @ w1[e]) * (xblk @ w3[e])
        return hid @ w2[e]

    yb = lax.map(expert_block, (xb, block_e))
    y = yb.reshape(n_slots, d).astype(jnp.float32) * slot_w[:, None]
    out = jax.ops.segment_sum(y, slot_tok, num_segments=n_tok)
    return out.reshape(bn, s, d)


def setup_inputs(seed: int = 0) -> dict:
    key = jax.random.key(seed)
    ks = jax.random.split(key, 26)
    f32 = jnp.float32
    D, L = D_MODEL, DEPTH

    def nrm(k, shape, scale):
        return jax.random.normal(k, shape, f32) * scale

    a0 = jax.random.uniform(ks[12], (L, LRU_WIDTH), f32, 0.9, 0.999)
    return {
        "x": jax.random.normal(ks[0], (BATCH, SEQ, D), f32),
        "c": jax.random.normal(ks[1], (BATCH, D), f32),
        "w_ada": nrm(ks[2], (L, D, 6 * D), 0.5 * D ** -0.5),
        "b_ada": nrm(ks[3], (L, 6 * D), 0.02),
        "g_mix": 1.0 + nrm(ks[4], (L, D), 0.1),
        "w_in": nrm(ks[5], (L, D, IN_COLS), D ** -0.5),
        "lru_conv_w": nrm(ks[6], (L, LRU_CONV, LRU_WIDTH), LRU_CONV ** -0.5),
        "lru_conv_b": nrm(ks[7], (L, LRU_WIDTH), 0.02),
        "lru_w_a": nrm(ks[8], (L, LRU_HEADS, LRU_HEAD_DIM, LRU_HEAD_DIM), LRU_HEAD_DIM ** -0.5),
        "lru_b_a": nrm(ks[9], (L, LRU_WIDTH), 0.02),
        "lru_w_x": nrm(ks[10], (L, LRU_HEADS, LRU_HEAD_DIM, LRU_HEAD_DIM), LRU_HEAD_DIM ** -0.5),
        "lru_b_x": nrm(ks[11], (L, LRU_WIDTH), 0.02),
        "lru_lambda": jnp.log(a0) - jnp.log1p(-a0),
        "lru_out_g": 1.0 + nrm(ks[13], (L, LRU_WIDTH), 0.1),
        "sc_conv_w": nrm(ks[14], (L, SC_CONV, SC_WIDTH), SC_CONV ** -0.5),
        "sc_out_g": 1.0 + nrm(ks[15], (L, SC_WIDTH), 0.1),
        "w_out": nrm(ks[16], (L, MIX_WIDTH, D), MIX_WIDTH ** -0.5),
        "g_ffn": 1.0 + nrm(ks[17], (L, D), 0.1),
        "w_router_group": nrm(ks[18], (L, D, N_GROUPS), D ** -0.5),
        "b_router_group": nrm(ks[19], (L, N_GROUPS), 0.01),
        "w_router_expert": nrm(ks[20], (L, D, N_EXPERTS), D ** -0.5),
        "b_router_expert": nrm(ks[21], (L, N_EXPERTS), 0.01),
        "w1": nrm(ks[22], (L, N_EXPERTS, D, D_FF_EXPERT), D ** -0.5),
        "w3": nrm(ks[23], (L, N_EXPERTS, D, D_FF_EXPERT), D ** -0.5),
        "w2": nrm(ks[24], (L, N_EXPERTS, D_FF_EXPERT, D), D_FF_EXPERT ** -0.5),
        "g_final": 1.0 + nrm(ks[25], (D,), 0.1),
    }


def reference(x, c, w_ada, b_ada, g_mix, w_in, lru_conv_w, lru_conv_b, lru_w_a, lru_b_a,
              lru_w_x, lru_b_x, lru_lambda, lru_out_g, sc_conv_w, sc_out_g, w_out, g_ffn,
              w_router_group, b_router_group, w_router_expert, b_router_expert,
              w1, w3, w2, g_final):
    out_dtype = x.dtype
    cond = jax.nn.silu(c.astype(jnp.float32))
    split_cols = [LRU_WIDTH, 2 * LRU_WIDTH, 2 * LRU_WIDTH + SC_WIDTH, 2 * LRU_WIDTH + 2 * SC_WIDTH]
    for l in range(DEPTH):
        mod = cond @ w_ada[l] + b_ada[l]
        sh1, sc1, gt1, sh2, sc2, gt2 = jnp.split(mod, 6, axis=-1)

        h = modulate(rms_norm(x, g_mix[l]), sh1, sc1)
        proj = h @ w_in[l]
        x_lru, gate_lru, b_sc, c_sc, x_sc = jnp.split(proj, split_cols, axis=-1)

        x_lru = causal_dwconv(x_lru, lru_conv_w[l]) + lru_conv_b[l]
        h_lru = rg_lru(x_lru.astype(jnp.float32), lru_w_a[l], lru_b_a[l],
                       lru_w_x[l], lru_b_x[l], lru_lambda[l])
        y_lru = group_rms_norm(h_lru * jax.nn.gelu(gate_lru), lru_out_g[l], LRU_HEADS)

        y_sc = b_sc * causal_dwconv(c_sc * x_sc, sc_conv_w[l])
        y_sc = group_rms_norm(y_sc, sc_out_g[l], SC_GROUPS)

        mix = jnp.concatenate([y_lru, y_sc], axis=-1) @ w_out[l]
        x = x + gt1[:, None, :] * mix

        h2 = modulate(rms_norm(x, g_ffn[l]), sh2, sc2)
        ffn = hier_moe(h2, w_router_group[l], b_router_group[l], w_router_expert[l],
                       b_router_expert[l], w1[l], w3[l], w2[l])
        x = x + gt2[:, None, :] * ffn
    return rms_norm(x, g_final).astype(out_dtype)
```

```python
import functools

import jax
import jax.numpy as jnp
from jax import lax
from jax.experimental import pallas as pl
from jax.experimental.pallas import tpu as pltpu

EPS = 1e-6
LRU_C = 8.0
LRU_HEADS = 4
SC_GROUPS = 8
N_GROUPS = 4
EXPERTS_PER_GROUP = 8
TOP_K = 2

SUBLANES = 8
LANES = 128
VMEM_LIMIT_BYTES = 56 * 1024 * 1024

ADA_BLOCK_COLS = 1536
MIX_ROWS = 256
DISP_ROWS = 256
MOE_ROWS = 256
FIN_ROWS = 256
ROUTE_ROWS = 8 + N_GROUPS * EXPERTS_PER_GROUP


def _const_spec(shape):
    nd = len(shape)
    return pl.BlockSpec(shape, lambda *_: (0,) * nd, pipeline_mode=pl.Buffered(1))


def _dot(a, b):
    return jnp.dot(a, b, preferred_element_type=jnp.float32)


def _ada_kernel(c_ref, w_ref, b_ref, o_ref):
    cond = jax.nn.silu(c_ref[...]).astype(jnp.bfloat16)
    o_ref[...] = _dot(cond, w_ref[...].astype(jnp.bfloat16)) + b_ref[...]


def _ada(c_pad, w_ada, b_ada):
    rows, d = c_pad.shape
    cols = w_ada.shape[1]
    assert cols % ADA_BLOCK_COLS == 0
    return pl.pallas_call(
        _ada_kernel,
        grid=(cols // ADA_BLOCK_COLS,),
        in_specs=[
            pl.BlockSpec((rows, d), lambda j: (0, 0)),
            pl.BlockSpec((d, ADA_BLOCK_COLS), lambda j: (0, j)),
            pl.BlockSpec((1, ADA_BLOCK_COLS), lambda j: (0, j)),
        ],
        out_specs=pl.BlockSpec((rows, ADA_BLOCK_COLS), lambda j: (0, j)),
        out_shape=jax.ShapeDtypeStruct((rows, cols), jnp.float32),
        compiler_params=pltpu.CompilerParams(
            dimension_semantics=("arbitrary",), vmem_limit_bytes=VMEM_LIMIT_BYTES),
        name="ada",
    )(c_pad, w_ada, b_ada)


def _shift_rows(x, halo, s):
    if s == 0:
        return x
    rolled = pltpu.roll(x, s, 0)
    row = lax.broadcasted_iota(jnp.int32, (SUBLANES, x.shape[1]), 0)
    top = jnp.where(row < s, pltpu.roll(halo, s, 0), rolled[0:SUBLANES])
    return jnp.concatenate([top, rolled[SUBLANES:]], axis=0)


def _causal_conv(x, halo, w_ref, width):
    y = _shift_rows(x, halo, width - 1) * w_ref[0:1, :]
    for k in range(1, width):
        y = y + _shift_rows(x, halo, width - 1 - k) * w_ref[k:k + 1, :]
    return y


def _group_rms(y, g_ref, n_groups):
    t, ch = y.shape
    gw = ch // n_groups
    outs = []
    for g in range(n_groups):
        yg = y[:, g * gw:(g + 1) * gw]
        ms = jnp.mean(yg * yg, axis=-1, keepdims=True)
        outs.append(yg * lax.rsqrt(ms + EPS))
    return jnp.concatenate(outs, axis=-1) * g_ref[...]


def _linear_scan(a, u, h0):
    t, ch = a.shape
    r8 = lax.broadcasted_iota(jnp.int32, (1, SUBLANES, ch), 1)
    for d in (1, 2, 4):
        keep = r8 >= d
        a_sh = pltpu.roll(a, d, 0)
        u_sh = pltpu.roll(u, d, 0)
        u_new = (a * u_sh + u).reshape(t // SUBLANES, SUBLANES, ch)
        a_new = (a * a_sh).reshape(t // SUBLANES, SUBLANES, ch)
        u = jnp.where(keep, u_new, u.reshape(t // SUBLANES, SUBLANES, ch)).reshape(t, ch)
        a = jnp.where(keep, a_new, a.reshape(t // SUBLANES, SUBLANES, ch)).reshape(t, ch)
    carry = h0
    outs = []
    for j in range(t // SUBLANES):
        hj = u[j * SUBLANES:(j + 1) * SUBLANES] + a[j * SUBLANES:(j + 1) * SUBLANES] * carry
        outs.append(hj)
        carry = hj[SUBLANES - 1:SUBLANES]
    return jnp.concatenate(outs, axis=0), carry


def _first_argmax(v, vmax, n):
    idx = lax.broadcasted_iota(jnp.int32, v.shape, 0)
    return jnp.min(jnp.where(v == vmax, idx, n), axis=0, keepdims=True)


def _mixer_kernel(x_ref, mod_ref, g_mix_ref, w_in_ref, lcw_ref, lcb_ref, wa_ref, ba_ref, wx_ref, bx_ref,
                  lam_ref, log_ref, scw_ref, sog_ref, w_out_ref, g_ffn_ref, wr_ref, br_ref, tri_ref,
                  x1_ref, h2_ref, ri_ref, rg_ref, cnt_ref,
                  lru_halo, sc_halo, h_state, base):
    b = pl.program_id(0)
    t = pl.program_id(1)
    d = x_ref.shape[2]
    rows = x_ref.shape[1]
    lw = lam_ref.shape[1]
    sw = scw_ref.shape[1]
    hd = lw // LRU_HEADS

    @pl.when(t == 0)
    def _():
        lru_halo[...] = jnp.zeros_like(lru_halo)
        sc_halo[...] = jnp.zeros_like(sc_halo)
        h_state[...] = jnp.zeros_like(h_state)

    @pl.when((t == 0) & (b == 0))
    def _():
        base[...] = jnp.zeros_like(base)

    mod = mod_ref[0]
    sh1, sc1, gt1 = mod[:, 0:d], mod[:, d:2 * d], mod[:, 2 * d:3 * d]
    sh2, sc2 = mod[:, 3 * d:4 * d], mod[:, 4 * d:5 * d]

    xt = x_ref[0]
    ms = jnp.mean(xt * xt, axis=-1, keepdims=True)
    h = xt * lax.rsqrt(ms + EPS) * g_mix_ref[...]
    h = (h * (1.0 + sc1) + sh1).astype(jnp.bfloat16)

    x_lru = _dot(h, w_in_ref[:, 0:lw])
    gate = _dot(h, w_in_ref[:, lw:2 * lw])
    b_sc = _dot(h, w_in_ref[:, 2 * lw:2 * lw + sw])
    c_sc = _dot(h, w_in_ref[:, 2 * lw + sw:2 * lw + 2 * sw])
    x_sc = _dot(h, w_in_ref[:, 2 * lw + 2 * sw:2 * lw + 3 * sw])

    xc = _causal_conv(x_lru, lru_halo[...], lcw_ref, lcw_ref.shape[0]) + lcb_ref[...]
    lru_halo[...] = x_lru[rows - SUBLANES:rows]
    xcb = xc.astype(jnp.bfloat16)
    ra = jnp.concatenate([_dot(xcb[:, i * hd:(i + 1) * hd], wa_ref[i]) for i in range(LRU_HEADS)], axis=-1)
    rx = jnp.concatenate([_dot(xcb[:, i * hd:(i + 1) * hd], wx_ref[i]) for i in range(LRU_HEADS)], axis=-1)
    r = jax.nn.sigmoid(ra + ba_ref[...])
    ig = jax.nn.sigmoid(rx + bx_ref[...])
    nl = -lam_ref[...]
    softplus = jnp.maximum(nl, 0.0) + jnp.log1p(jnp.exp(-jnp.abs(nl)))
    log_a = (-LRU_C) * r * softplus
    a = jnp.exp(log_a)
    th = jnp.tanh(log_a)
    mult = jnp.sqrt(-2.0 * th / (1.0 - th))
    hl, carry = _linear_scan(a, mult * (ig * xc), h_state[...])
    h_state[...] = carry
    y_lru = _group_rms(hl * jax.nn.gelu(gate), log_ref, LRU_HEADS)

    cx = c_sc * x_sc
    y_sc = b_sc * _causal_conv(cx, sc_halo[...], scw_ref, scw_ref.shape[0])
    sc_halo[...] = cx[rows - SUBLANES:rows]
    y_sc = _group_rms(y_sc, sog_ref, SC_GROUPS)

    mix = (_dot(y_lru.astype(jnp.bfloat16), w_out_ref[0:lw, :])
           + _dot(y_sc.astype(jnp.bfloat16), w_out_ref[lw:lw + sw, :]))
    x1 = xt + gt1 * mix
    x1_ref[0] = x1

    ms2 = jnp.mean(x1 * x1, axis=-1, keepdims=True)
    h2 = x1 * lax.rsqrt(ms2 + EPS) * g_ffn_ref[...]
    h2 = h2 * (1.0 + sc2) + sh2
    hi = lax.bitcast_convert_type(h2[:, 0:d // 2].astype(jnp.bfloat16).astype(jnp.float32), jnp.uint32)
    lo = lax.bitcast_convert_type(h2[:, d // 2:d].astype(jnp.bfloat16).astype(jnp.float32), jnp.uint32)
    h2_ref[0] = hi | (lo >> 16)

    lt = lax.dot_general(wr_ref[...], h2, (((1,), (1,)), ((), ())),
                         precision=lax.Precision.HIGHEST,
                         preferred_element_type=jnp.float32) + br_ref[...]
    gl = lt[0:N_GROUPS]
    gmax = jnp.max(gl, axis=0, keepdims=True)
    gsum = jnp.sum(jnp.exp(gl - gmax), axis=0, keepdims=True)
    g_p = 1.0 / gsum
    g_idx = _first_argmax(gl, gmax, N_GROUPS)
    el = lt[SUBLANES:SUBLANES + N_GROUPS * EXPERTS_PER_GROUP]
    sel = jnp.zeros((EXPERTS_PER_GROUP, rows), jnp.float32)
    for g in range(N_GROUPS):
        sel = sel + jnp.where(g_idx == g, el[g * EXPERTS_PER_GROUP:(g + 1) * EXPERTS_PER_GROUP], 0.0)
    emax = jnp.max(sel, axis=0, keepdims=True)
    ee = jnp.exp(sel - emax)
    ep = ee / jnp.sum(ee, axis=0, keepdims=True)
    p1 = jnp.max(ep, axis=0, keepdims=True)
    i1 = _first_argmax(ep, p1, EXPERTS_PER_GROUP)
    eidx = lax.broadcasted_iota(jnp.int32, ep.shape, 0)
    rest = jnp.where(eidx == i1, -1.0, ep)
    p2 = jnp.max(rest, axis=0, keepdims=True)
    i2 = _first_argmax(rest, p2, EXPERTS_PER_GROUP)
    psum = p1 + p2
    e1 = g_idx * EXPERTS_PER_GROUP + i1
    e2 = g_idx * EXPERTS_PER_GROUP + i2

    n_exp = N_GROUPS * EXPERTS_PER_GROUP
    xidx = lax.broadcasted_iota(jnp.int32, (n_exp, rows), 0)
    oh1 = xidx == e1
    oh2 = xidx == e2
    cnt = jnp.where(oh1 | oh2, 1.0, 0.0)
    before = _dot(cnt.astype(jnp.bfloat16), tri_ref[...]) + base[:, 0:1]
    rank1 = jnp.sum(jnp.where(oh1, before, 0.0), axis=0, keepdims=True)
    rank2 = jnp.sum(jnp.where(oh2, before, 0.0), axis=0, keepdims=True)
    base[...] = base[...] + jnp.sum(cnt, axis=1, keepdims=True)
    cnt_ref[...] = base[...]

    zi = jnp.zeros((SUBLANES - 4, rows), jnp.int32)
    ri_ref[0] = jnp.concatenate([e1, e2, rank1.astype(jnp.int32), rank2.astype(jnp.int32), zi], axis=0)
    zf = jnp.zeros((SUBLANES - 2, rows), jnp.float32)
    rg_ref[0] = jnp.concatenate([g_p * (p1 / psum), g_p * (p2 / psum), zf], axis=0)


def _mixer(x, mod3, g_mix, w_in, lcw, lcb, wa, ba, wx, bx, lam, log, scw, sog, w_out, g_ffn, wr, br):
    bsz, seq, d = x.shape
    rows = MIX_ROWS
    assert seq % rows == 0
    lw = lam.shape[1]
    sw = scw.shape[1]
    n_exp = N_GROUPS * EXPERTS_PER_GROUP
    tri = (lax.broadcasted_iota(jnp.int32, (rows, rows), 0)
           < lax.broadcasted_iota(jnp.int32, (rows, rows), 1)).astype(jnp.bfloat16)
    consts = [g_mix, w_in, lcw, lcb, wa, ba, wx, bx, lam, log, scw, sog, w_out, g_ffn, wr, br, tri]
    tile = lambda bb, tt: (bb, tt, 0)
    lane_tile = lambda bb, tt: (bb, 0, tt)
    return pl.pallas_call(
        _mixer_kernel,
        grid=(bsz, seq // rows),
        in_specs=[pl.BlockSpec((1, rows, d), tile),
                  pl.BlockSpec((1, 1, mod3.shape[2]), lambda bb, tt: (bb, 0, 0))]
                 + [_const_spec(c.shape) for c in consts],
        out_specs=[pl.BlockSpec((1, rows, d), tile),
                   pl.BlockSpec((1, rows, d // 2), tile),
                   pl.BlockSpec((1, SUBLANES, rows), lane_tile),
                   pl.BlockSpec((1, SUBLANES, rows), lane_tile),
                   pl.BlockSpec((n_exp, LANES), lambda bb, tt: (0, 0))],
        out_shape=[jax.ShapeDtypeStruct((bsz, seq, d), jnp.float32),
                   jax.ShapeDtypeStruct((bsz, seq, d // 2), jnp.uint32),
                   jax.ShapeDtypeStruct((bsz, SUBLANES, seq), jnp.int32),
                   jax.ShapeDtypeStruct((bsz, SUBLANES, seq), jnp.float32),
                   jax.ShapeDtypeStruct((n_exp, LANES), jnp.float32)],
        scratch_shapes=[pltpu.VMEM((SUBLANES, lw), jnp.float32),
                        pltpu.VMEM((SUBLANES, sw), jnp.float32),
                        pltpu.VMEM((1, lw), jnp.float32),
                        pltpu.VMEM((n_exp, LANES), jnp.float32)],
        compiler_params=pltpu.CompilerParams(
            dimension_semantics=("arbitrary", "arbitrary"), vmem_limit_bytes=VMEM_LIMIT_BYTES),
        name="mixer",
    )(x, mod3, *consts)


def _row_copy(src, dst, sem):
    return pltpu.make_async_copy(src, dst, sem)


def _disp_kernel(dest_ref, h2_ref, xs_in_ref, xs_ref, sem):
    del xs_in_ref
    i = pl.program_id(0)
    rows = h2_ref.shape[0]

    def issue(j, _):
        for k in range(TOP_K):
            dst = dest_ref[(i * rows + j) * TOP_K + k]
            _row_copy(h2_ref.at[pl.ds(j, 1)], xs_ref.at[pl.ds(dst, 1)], sem).start()
        return 0

    lax.fori_loop(0, rows, issue, 0, unroll=8)

    def drain(j, _):
        for k in range(TOP_K):
            _row_copy(h2_ref.at[pl.ds(0, 1)], xs_ref.at[pl.ds(0, 1)], sem).wait()
        return 0

    lax.fori_loop(0, rows, drain, 0, unroll=8)


def _dispatch(dest, h2_rows, xs_init):
    n_tok, words = h2_rows.shape
    rows = DISP_ROWS
    assert n_tok % rows == 0
    return pl.pallas_call(
        _disp_kernel,
        grid_spec=pltpu.PrefetchScalarGridSpec(
            num_scalar_prefetch=1,
            grid=(n_tok // rows,),
            in_specs=[pl.BlockSpec((rows, words), lambda i, dest: (i, 0)),
                      pl.BlockSpec(memory_space=pl.ANY)],
            out_specs=pl.BlockSpec(memory_space=pl.ANY),
            scratch_shapes=[pltpu.SemaphoreType.DMA],
        ),
        out_shape=jax.ShapeDtypeStruct(xs_init.shape, xs_init.dtype),
        input_output_aliases={2: 0},
        compiler_params=pltpu.CompilerParams(
            dimension_semantics=("arbitrary",), vmem_limit_bytes=VMEM_LIMIT_BYTES),
        name="dispatch",
    )(dest, h2_rows, xs_init)


def _moe_kernel(block_e_ref, n_used_ref, xs_ref, w1_ref, w3_ref, w2_ref, y_ref):
    del block_e_ref
    half = xs_ref.shape[1]

    @pl.when(pl.program_id(0) < n_used_ref[0])
    def _():
        w = xs_ref[...]
        x_hi = lax.bitcast_convert_type(w & jnp.uint32(0xFFFF0000), jnp.float32).astype(jnp.bfloat16)
        x_lo = lax.bitcast_convert_type(w << 16, jnp.float32).astype(jnp.bfloat16)
        h1 = _dot(x_hi, w1_ref[0, 0:half, :]) + _dot(x_lo, w1_ref[0, half:2 * half, :])
        h3 = _dot(x_hi, w3_ref[0, 0:half, :]) + _dot(x_lo, w3_ref[0, half:2 * half, :])
        hid = (jax.nn.silu(h1) * h3).astype(jnp.bfloat16)
        y_ref[...] = _dot(hid, w2_ref[0])

    @pl.when(pl.program_id(0) >= n_used_ref[0])
    def _():
        y_ref[...] = jnp.zeros_like(y_ref)


def _moe(block_e, n_used, xs, w1, w3, w2):
    n_slots, half = xs.shape
    n_exp, d, f = w1.shape
    rows = MOE_ROWS
    n_blocks = n_slots // rows
    blk = lambda i, be, nu: (jnp.minimum(i, nu[0] - 1), 0)
    wsel = lambda i, be, nu: (be[i], 0, 0)
    return pl.pallas_call(
        _moe_kernel,
        grid_spec=pltpu.PrefetchScalarGridSpec(
            num_scalar_prefetch=2,
            grid=(n_blocks,),
            in_specs=[pl.BlockSpec((rows, half), blk),
                      pl.BlockSpec((1, d, f), wsel),
                      pl.BlockSpec((1, d, f), wsel),
                      pl.BlockSpec((1, f, d), wsel)],
            out_specs=pl.BlockSpec((rows, d), lambda i, be, nu: (i, 0)),
        ),
        out_shape=jax.ShapeDtypeStruct((n_slots, d), jnp.float32),
        compiler_params=pltpu.CompilerParams(
            dimension_semantics=("arbitrary",), vmem_limit_bytes=VMEM_LIMIT_BYTES),
        name="moe",
    )(block_e, n_used, xs, w1, w3, w2)


def _final_kernel(dest_ref, x1_ref, mod_ref, gate_ref, g_fin_ref, yb_ref, o_ref, ybuf, sem):
    i = pl.program_id(0) * pl.num_programs(1) + pl.program_id(1)
    rows = x1_ref.shape[1]
    d = x1_ref.shape[2]

    def issue(j, _):
        for k in range(TOP_K):
            src = dest_ref[(i * rows + j) * TOP_K + k]
            _row_copy(yb_ref.at[pl.ds(src, 1)], ybuf.at[k, pl.ds(j, 1)], sem).start()
        return 0

    lax.fori_loop(0, rows, issue, 0, unroll=8)

    def drain(j, _):
        for k in range(TOP_K):
            _row_copy(yb_ref.at[pl.ds(0, 1)], ybuf.at[k, pl.ds(0, 1)], sem).wait()
        return 0

    lax.fori_loop(0, rows, drain, 0, unroll=8)

    gt2 = mod_ref[0][:, 5 * d:6 * d]
    gates = gate_ref[...]
    ffn = ybuf[0] * gates[:, 0:1] + ybuf[1] * gates[:, 1:2]
    xo = x1_ref[0] + gt2 * ffn
    ms = jnp.mean(xo * xo, axis=-1, keepdims=True)
    o_ref[0] = (xo * lax.rsqrt(ms + EPS) * g_fin_ref[...]).astype(o_ref.dtype)


def _final(dest, x1, mod3, gates, g_final, yb, out_dtype):
    bsz, seq, d = x1.shape
    rows = FIN_ROWS
    assert seq % rows == 0
    per_b = seq // rows
    return pl.pallas_call(
        _final_kernel,
        grid_spec=pltpu.PrefetchScalarGridSpec(
            num_scalar_prefetch=1,
            grid=(bsz, per_b),
            in_specs=[pl.BlockSpec((1, rows, d), lambda b, t, dest: (b, t, 0)),
                      pl.BlockSpec((1, 1, mod3.shape[2]), lambda b, t, dest: (b, 0, 0)),
                      pl.BlockSpec((rows, TOP_K), lambda b, t, dest: (b * per_b + t, 0)),
                      pl.BlockSpec((1, d), lambda b, t, dest: (0, 0)),
                      pl.BlockSpec(memory_space=pl.ANY)],
            out_specs=pl.BlockSpec((1, rows, d), lambda b, t, dest: (b, t, 0)),
            scratch_shapes=[pltpu.VMEM((TOP_K, rows, d), jnp.float32),
                            pltpu.SemaphoreType.DMA],
        ),
        out_shape=jax.ShapeDtypeStruct((bsz, seq, d), out_dtype),
        compiler_params=pltpu.CompilerParams(
            dimension_semantics=("arbitrary", "arbitrary"), vmem_limit_bytes=VMEM_LIMIT_BYTES),
        name="final",
    )(dest, x1, mod3, gates, g_final, yb)


def kernel(x, c, w_ada, b_ada, g_mix, w_in, lru_conv_w, lru_conv_b, lru_w_a, lru_b_a, lru_w_x, lru_b_x,
           lru_lambda, lru_out_g, sc_conv_w, sc_out_g, w_out, g_ffn, w_router_group, b_router_group,
           w_router_expert, b_router_expert, w1, w3, w2, g_final):
    out_dtype = x.dtype
    bsz, seq, d = x.shape
    depth = w_ada.shape[0]
    assert depth == 1, "the final kernel fuses the last layer's residual add with the final norm"
    n_tok = bsz * seq
    n_exp = w1.shape[1]
    assert n_exp == N_GROUPS * EXPERTS_PER_GROUP
    bf = jnp.bfloat16
    row = lambda v: v.reshape(1, -1)

    c_pad = jnp.pad(c.astype(jnp.float32), ((0, 2 * SUBLANES - bsz), (0, 0)))
    n_blocks = -(-n_tok * TOP_K // MOE_ROWS) + n_exp
    n_slots = n_blocks * MOE_ROWS

    x = x.astype(jnp.float32)
    for l in range(depth):
        mod = _ada(c_pad, w_ada[l], row(b_ada[l]))[:bsz]
        mod3 = mod.reshape(bsz, 1, 6 * d)

        wr = jnp.concatenate([w_router_group[l].T,
                              jnp.zeros((SUBLANES - N_GROUPS, d), jnp.float32),
                              w_router_expert[l].T], axis=0)
        br = jnp.concatenate([b_router_group[l], jnp.zeros((SUBLANES - N_GROUPS,), jnp.float32),
                              b_router_expert[l]]).reshape(ROUTE_ROWS, 1)
        x1, h2, ri, rg, cnt = _mixer(
            x, mod3, row(g_mix[l]), w_in[l].astype(bf), lru_conv_w[l], row(lru_conv_b[l]),
            lru_w_a[l].astype(bf), row(lru_b_a[l]), lru_w_x[l].astype(bf), row(lru_b_x[l]),
            row(lru_lambda[l]), row(lru_out_g[l]), sc_conv_w[l], row(sc_out_g[l]), w_out[l].astype(bf),
            row(g_ffn[l]), wr, br)

        counts = cnt[:, 0].astype(jnp.int32)
        padded = ((counts + MOE_ROWS - 1) // MOE_ROWS) * MOE_ROWS
        pends = jnp.cumsum(padded)
        pstarts = pends - padded
        e_tk = jnp.transpose(ri[:, 0:TOP_K, :], (0, 2, 1)).reshape(n_tok * TOP_K)
        r_tk = jnp.transpose(ri[:, TOP_K:2 * TOP_K, :], (0, 2, 1)).reshape(n_tok * TOP_K)
        dest = pstarts[e_tk] + r_tk
        gates = jnp.transpose(rg[:, 0:TOP_K, :], (0, 2, 1)).reshape(n_tok, TOP_K)
        n_used = (pends[-1] // MOE_ROWS).astype(jnp.int32)
        blk_start = jnp.minimum(jnp.arange(n_blocks, dtype=jnp.int32), n_used - 1) * MOE_ROWS
        block_e = jnp.minimum(jnp.searchsorted(pends, blk_start, side='right'), n_exp - 1).astype(jnp.int32)

        xs = _dispatch(dest, h2.reshape(n_tok, d // 2), jnp.zeros((n_slots, d // 2), jnp.uint32))
        yb = _moe(block_e, n_used.reshape(1), xs, w1[l].astype(bf), w3[l].astype(bf), w2[l].astype(bf))
        x = _final(dest, x1, mod3, gates, row(g_final), yb, out_dtype)
    return x
```

```python
import jax
import jax.numpy as jnp
from jax import lax
from jax.experimental import pallas as pl
from jax.experimental.pallas import tpu as pltpu

EPS = 1e-6
LRU_C = 8.0
LRU_HEADS = 4
SC_GROUPS = 8
N_GROUPS = 4
EXPERTS_PER_GROUP = 8
N_EXPERTS = N_GROUPS * EXPERTS_PER_GROUP
TOP_K = 2

SUBLANES = 8
LANES = 128
VMEM_LIMIT_BYTES = 60 * 1024 * 1024

ADA_BLOCK_COLS = 1536
PACK_BLOCK_ROWS = 512
MIX_ROWS = 256
DISP_ROWS = 256
MOE_ROWS = 256
FIN_ROWS = 256
ROUTE_ROWS = SUBLANES + N_EXPERTS
RANK_BITS = 20
RANK_MASK = (1 << RANK_BITS) - 1


def _const_spec(shape):
    nd = len(shape)
    return pl.BlockSpec(shape, lambda *_: (0,) * nd, pipeline_mode=pl.Buffered(1))


def _dot(a, b):
    return jnp.dot(a, b, preferred_element_type=jnp.float32)


def _pack_rows(w):
    return pltpu.bitcast(w.astype(jnp.bfloat16), jnp.uint32)


def _unpack_rows(p):
    return pltpu.bitcast(p, jnp.bfloat16)


def _params(n_axes):
    return pltpu.CompilerParams(dimension_semantics=("arbitrary",) * n_axes,
                                vmem_limit_bytes=VMEM_LIMIT_BYTES)


def _pack_kernel(w_ref, o_ref):
    o_ref[...] = _pack_rows(w_ref[...])


def _pack_weight(w2d):
    r, c = w2d.shape
    br = min(PACK_BLOCK_ROWS, r)
    assert r % br == 0
    return pl.pallas_call(
        _pack_kernel,
        grid=(r // br,),
        in_specs=[pl.BlockSpec((br, c), lambda i: (i, 0))],
        out_specs=pl.BlockSpec((br // 2, c), lambda i: (i, 0)),
        out_shape=jax.ShapeDtypeStruct((r // 2, c), jnp.uint32),
        compiler_params=_params(1),
        name="pack",
    )(w2d)


def _ada_kernel(c_ref, w_ref, b_ref, o_ref):
    cond = jax.nn.silu(c_ref[...]).astype(jnp.bfloat16)
    o_ref[...] = _dot(cond, w_ref[...].astype(jnp.bfloat16)) + b_ref[...]


def _ada(c_pad, w_ada, b_ada):
    rows, d = c_pad.shape
    cols = w_ada.shape[1]
    assert cols % ADA_BLOCK_COLS == 0
    return pl.pallas_call(
        _ada_kernel,
        grid=(cols // ADA_BLOCK_COLS,),
        in_specs=[
            pl.BlockSpec((rows, d), lambda j: (0, 0)),
            pl.BlockSpec((d, ADA_BLOCK_COLS), lambda j: (0, j)),
            pl.BlockSpec((1, ADA_BLOCK_COLS), lambda j: (0, j)),
        ],
        out_specs=pl.BlockSpec((rows, ADA_BLOCK_COLS), lambda j: (0, j)),
        out_shape=jax.ShapeDtypeStruct((rows, cols), jnp.float32),
        compiler_params=_params(1),
        name="ada",
    )(c_pad, w_ada, b_ada)


def _shift_rows(x, halo, s):
    if s == 0:
        return x
    rolled = pltpu.roll(x, s, 0)
    row = lax.broadcasted_iota(jnp.int32, (SUBLANES, x.shape[1]), 0)
    top = jnp.where(row < s, pltpu.roll(halo, s, 0), rolled[0:SUBLANES])
    return jnp.concatenate([top, rolled[SUBLANES:]], axis=0)


def _causal_conv(x, halo, w_ref, width):
    y = _shift_rows(x, halo, width - 1) * w_ref[0:1, :]
    for k in range(1, width):
        y = y + _shift_rows(x, halo, width - 1 - k) * w_ref[k:k + 1, :]
    return y


def _group_rms(y, g_ref, n_groups):
    gw = y.shape[1] // n_groups
    outs = []
    for g in range(n_groups):
        yg = y[:, g * gw:(g + 1) * gw]
        ms = jnp.mean(yg * yg, axis=-1, keepdims=True)
        outs.append(yg * lax.rsqrt(ms + EPS))
    return jnp.concatenate(outs, axis=-1) * g_ref[...]


def _linear_scan(a, u, h0):
    t, ch = a.shape
    r8 = lax.broadcasted_iota(jnp.int32, (1, SUBLANES, ch), 1)
    for d in (1, 2, 4):
        keep = r8 >= d
        a_sh = pltpu.roll(a, d, 0)
        u_sh = pltpu.roll(u, d, 0)
        u_new = (a * u_sh + u).reshape(t // SUBLANES, SUBLANES, ch)
        a_new = (a * a_sh).reshape(t // SUBLANES, SUBLANES, ch)
        u = jnp.where(keep, u_new, u.reshape(t // SUBLANES, SUBLANES, ch)).reshape(t, ch)
        a = jnp.where(keep, a_new, a.reshape(t // SUBLANES, SUBLANES, ch)).reshape(t, ch)
    carry = h0
    outs = []
    for j in range(t // SUBLANES):
        hj = u[j * SUBLANES:(j + 1) * SUBLANES] + a[j * SUBLANES:(j + 1) * SUBLANES] * carry
        outs.append(hj)
        carry = hj[SUBLANES - 1:SUBLANES]
    return jnp.concatenate(outs, axis=0), carry


def _first_argmax(v, vmax, n):
    idx = lax.broadcasted_iota(jnp.int32, v.shape, 0)
    return jnp.min(jnp.where(v == vmax, idx, n), axis=0, keepdims=True)


def _mixer_kernel(x_ref, mod_ref, w1raw_ref, g_mix_ref, w_in_ref, lcw_ref, lcb_ref, wa_ref, ba_ref, wx_ref,
                  bx_ref, lam_ref, log_ref, scw_ref, sog_ref, w_out_ref, g_ffn_ref, wr_ref, br_ref, tri_ref,
                  x1_ref, h2_ref, pk_ref, gr_ref, cnt_ref, w1p_ref,
                  lru_halo, sc_halo, h_state, lru_halo_in, sc_halo_in, h_state_in, lt_keep, base):
    b = pl.program_id(0)
    t = pl.program_id(1)
    last = pl.num_programs(1) - 1
    d = x_ref.shape[2]
    rows = x_ref.shape[1]
    lw = lam_ref.shape[1]
    sw = scw_ref.shape[1]
    hd = lw // LRU_HEADS
    cols = (0, lw, 2 * lw, 2 * lw + sw, 2 * lw + 2 * sw, 2 * lw + 3 * sw)

    w1p_ref[...] = _pack_rows(w1raw_ref[...])

    @pl.when(t == 0)
    def _():
        lru_halo[...] = jnp.zeros_like(lru_halo)
        sc_halo[...] = jnp.zeros_like(sc_halo)
        h_state[...] = jnp.zeros_like(h_state)
        lru_halo_in[...] = jnp.zeros_like(lru_halo_in)
        sc_halo_in[...] = jnp.zeros_like(sc_halo_in)
        h_state_in[...] = jnp.zeros_like(h_state_in)

    @pl.when((t == 0) & (b == 0))
    def _():
        base[...] = jnp.zeros_like(base)
        lt_keep[...] = jnp.zeros_like(lt_keep)

    mod = mod_ref[0]
    sh1, sc1, gt1 = mod[:, 0:d], mod[:, d:2 * d], mod[:, 2 * d:3 * d]
    sh2, sc2 = mod[:, 3 * d:4 * d], mod[:, 4 * d:5 * d]

    live = (t >= 1).astype(jnp.float32)
    lt = jnp.transpose(lt_keep[...]) + br_ref[...]
    gl = lt[0:N_GROUPS]
    gmax = jnp.max(gl, axis=0, keepdims=True)
    gsum = jnp.sum(jnp.exp(gl - gmax), axis=0, keepdims=True)
    g_p = 1.0 / gsum
    g_idx = _first_argmax(gl, gmax, N_GROUPS)
    el = lt[SUBLANES:SUBLANES + N_EXPERTS]
    sel = jnp.zeros((EXPERTS_PER_GROUP, rows), jnp.float32)
    for g in range(N_GROUPS):
        sel = sel + jnp.where(g_idx == g, el[g * EXPERTS_PER_GROUP:(g + 1) * EXPERTS_PER_GROUP], 0.0)
    emax = jnp.max(sel, axis=0, keepdims=True)
    ee = jnp.exp(sel - emax)
    ep = ee / jnp.sum(ee, axis=0, keepdims=True)
    p1 = jnp.max(ep, axis=0, keepdims=True)
    i1 = _first_argmax(ep, p1, EXPERTS_PER_GROUP)
    eidx = lax.broadcasted_iota(jnp.int32, ep.shape, 0)
    rest = jnp.where(eidx == i1, -1.0, ep)
    p2 = jnp.max(rest, axis=0, keepdims=True)
    i2 = _first_argmax(rest, p2, EXPERTS_PER_GROUP)
    psum = p1 + p2
    e1 = g_idx * EXPERTS_PER_GROUP + i1
    e2 = g_idx * EXPERTS_PER_GROUP + i2

    xidx = lax.broadcasted_iota(jnp.int32, (N_EXPERTS, rows), 0)
    oh1 = xidx == e1
    oh2 = xidx == e2
    cnt = jnp.where(oh1 | oh2, live, 0.0)
    g_lanes = jnp.concatenate([g_p * (p1 / psum), g_p * (p2 / psum),
                               jnp.zeros((LANES - TOP_K, rows), jnp.float32)], axis=0)
    gr_ref[0] = jnp.transpose(g_lanes)

    redo = t == last
    halo_l = jnp.where(redo, lru_halo_in[...], lru_halo[...])
    halo_s = jnp.where(redo, sc_halo_in[...], sc_halo[...])
    h_in = jnp.where(redo, h_state_in[...], h_state[...])
    lru_halo_in[...] = halo_l
    sc_halo_in[...] = halo_s
    h_state_in[...] = h_in

    xt = x_ref[0]
    ms = jnp.mean(xt * xt, axis=-1, keepdims=True)
    h = xt * lax.rsqrt(ms + EPS) * g_mix_ref[...]
    h = (h * (1.0 + sc1) + sh1).astype(jnp.bfloat16)

    def in_proj(i):
        return _dot(h, _unpack_rows(w_in_ref[:, cols[i]:cols[i + 1]]))

    x_lru = in_proj(0)
    xc = _causal_conv(x_lru, halo_l, lcw_ref, lcw_ref.shape[0]) + lcb_ref[...]
    lru_halo[...] = x_lru[rows - SUBLANES:rows]
    xcb = xc.astype(jnp.bfloat16)
    ra = jnp.concatenate([_dot(xcb[:, i * hd:(i + 1) * hd], _unpack_rows(wa_ref[i]))
                          for i in range(LRU_HEADS)], axis=-1)
    rx = jnp.concatenate([_dot(xcb[:, i * hd:(i + 1) * hd], _unpack_rows(wx_ref[i]))
                          for i in range(LRU_HEADS)], axis=-1)
    c_sc = in_proj(3)
    r = jax.nn.sigmoid(ra + ba_ref[...])
    ig = jax.nn.sigmoid(rx + bx_ref[...])
    nl = -lam_ref[...]
    softplus = jnp.maximum(nl, 0.0) + jnp.log1p(jnp.exp(-jnp.abs(nl)))
    log_a = (-LRU_C) * r * softplus
    a = jnp.exp(log_a)
    th = jnp.tanh(log_a)
    mult = jnp.sqrt(-2.0 * th / (1.0 - th))
    x_sc = in_proj(4)
    hl, carry = _linear_scan(a, mult * (ig * xc), h_in)
    h_state[...] = carry
    gate = in_proj(1)
    y_lru = _group_rms(hl * jax.nn.gelu(gate), log_ref, LRU_HEADS)
    mix_lru = _dot(y_lru.astype(jnp.bfloat16), _unpack_rows(w_out_ref[0:lw // 2, :]))

    b_sc = in_proj(2)
    cx = c_sc * x_sc
    y_sc = b_sc * _causal_conv(cx, halo_s, scw_ref, scw_ref.shape[0])
    sc_halo[...] = cx[rows - SUBLANES:rows]
    y_sc = _group_rms(y_sc, sog_ref, SC_GROUPS)

    mix = mix_lru + _dot(y_sc.astype(jnp.bfloat16), _unpack_rows(w_out_ref[lw // 2:(lw + sw) // 2, :]))
    x1 = xt + gt1 * mix
    x1_ref[0] = x1

    ms2 = jnp.mean(x1 * x1, axis=-1, keepdims=True)
    h2 = x1 * lax.rsqrt(ms2 + EPS) * g_ffn_ref[...]
    h2 = h2 * (1.0 + sc2) + sh2
    h2_ref[0] = h2
    h2_hi = h2.astype(jnp.bfloat16)
    h2_lo = (h2 - h2_hi.astype(jnp.float32)).astype(jnp.bfloat16)
    lt_keep[...] = (_dot(h2_hi, wr_ref[0]) + _dot(h2_lo, wr_ref[0])) + _dot(h2_hi, wr_ref[1])

    before = _dot(cnt.astype(jnp.bfloat16), tri_ref[...]) + base[:, 0:1]
    rank1 = jnp.sum(jnp.where(oh1, before, 0.0), axis=0, keepdims=True).astype(jnp.int32)
    rank2 = jnp.sum(jnp.where(oh2, before, 0.0), axis=0, keepdims=True).astype(jnp.int32)
    base[...] = base[...] + jnp.sum(cnt, axis=1, keepdims=True)
    cnt_ref[...] = base[...]
    pk_ref[0] = jnp.concatenate([(e1 << RANK_BITS) | rank1, (e2 << RANK_BITS) | rank2], axis=0)


def _mixer(x, mod3, w1raw, g_mix, w_in, lcw, lcb, wa, ba, wx, bx, lam, log, scw, sog, w_out, g_ffn, wr, br):
    bsz, seq, d = x.shape
    rows = MIX_ROWS
    assert seq % rows == 0
    per_b = seq // rows
    steps = bsz * per_b
    lw = lam.shape[1]
    sw = scw.shape[1]
    wrows, wcols = w1raw.shape
    assert wrows % (2 * SUBLANES * steps) == 0
    wblk = wrows // steps
    tri = (lax.broadcasted_iota(jnp.int32, (rows, rows), 0)
           < lax.broadcasted_iota(jnp.int32, (rows, rows), 1)).astype(jnp.bfloat16)
    consts = [g_mix, w_in, lcw, lcb, wa, ba, wx, bx, lam, log, scw, sog, w_out, g_ffn, wr, br, tri]
    tile = lambda bb, tt: (bb, jnp.minimum(tt, per_b - 1), 0)
    routed = lambda bb, tt: (bb, jnp.maximum(tt - 1, 0), 0)
    routed_lanes = lambda bb, tt: (bb, 0, jnp.maximum(tt - 1, 0))
    step = lambda bb, tt: (bb * per_b + jnp.minimum(tt, per_b - 1), 0)
    return pl.pallas_call(
        _mixer_kernel,
        grid=(bsz, per_b + 1),
        in_specs=[pl.BlockSpec((1, rows, d), tile),
                  pl.BlockSpec((1, 1, mod3.shape[2]), lambda bb, tt: (bb, 0, 0)),
                  pl.BlockSpec((wblk, wcols), step)]
                 + [_const_spec(c.shape) for c in consts],
        out_specs=[pl.BlockSpec((1, rows, d), tile),
                   pl.BlockSpec((1, rows, d), tile),
                   pl.BlockSpec((1, TOP_K, rows), routed_lanes),
                   pl.BlockSpec((1, rows, LANES), routed),
                   pl.BlockSpec((N_EXPERTS, LANES), lambda bb, tt: (0, 0)),
                   pl.BlockSpec((wblk // 2, wcols), step)],
        out_shape=[jax.ShapeDtypeStruct((bsz, seq, d), jnp.float32),
                   jax.ShapeDtypeStruct((bsz, seq, d), jnp.float32),
                   jax.ShapeDtypeStruct((bsz, TOP_K, seq), jnp.int32),
                   jax.ShapeDtypeStruct((bsz, seq, LANES), jnp.float32),
                   jax.ShapeDtypeStruct((N_EXPERTS, LANES), jnp.float32),
                   jax.ShapeDtypeStruct((wrows // 2, wcols), jnp.uint32)],
        scratch_shapes=[pltpu.VMEM((SUBLANES, lw), jnp.float32),
                        pltpu.VMEM((SUBLANES, sw), jnp.float32),
                        pltpu.VMEM((1, lw), jnp.float32)] * 2
                       + [pltpu.VMEM((rows, LANES), jnp.float32),
                          pltpu.VMEM((N_EXPERTS, LANES), jnp.float32)],
        compiler_params=_params(2),
        name="mixer",
    )(x, mod3, w1raw, *consts)


def _row_copy(src, dst, sem):
    return pltpu.make_async_copy(src, dst, sem)


def _slot_of(pk_ref, ps_ref, idx):
    p = pk_ref[idx]
    return ps_ref[lax.shift_right_logical(p, RANK_BITS)] + (p & RANK_MASK)


def _assignment_index(b, t, rows, seq, j, k):
    return (b * TOP_K + k) * seq + t * rows + j


def _disp_kernel(pk_ref, ps_ref, meta_ref, h2_ref, w3raw_ref, w2raw_ref, xs_ref, w3p_ref, w2p_ref,
                 zbuf, sem, sem_pad):
    b = pl.program_id(0)
    t = pl.program_id(1)
    i = b * pl.num_programs(1) + t
    rows = h2_ref.shape[0]
    seq = rows * pl.num_programs(1)
    block_rows = zbuf.shape[0]
    n_blocks = xs_ref.shape[0] // block_rows

    w3p_ref[...] = _pack_rows(w3raw_ref[...])
    w2p_ref[...] = _pack_rows(w2raw_ref[...])

    @pl.when(i == 0)
    def _():
        zbuf[...] = jnp.zeros_like(zbuf)

    pad_start = meta_ref[jnp.minimum(i, N_EXPERTS - 1)]
    pad_len = meta_ref[N_EXPERTS + jnp.minimum(i, N_EXPERTS - 1)]
    n_used = meta_ref[2 * N_EXPERTS]

    def pad_copies(fn):
        @pl.when(i < N_EXPERTS)
        def _():
            def body(j, _):
                fn(_row_copy(zbuf.at[pl.ds(0, 1)], xs_ref.at[pl.ds(pad_start + j, 1)], sem_pad))
                return 0
            lax.fori_loop(0, pad_len, body, 0)

    pad_copies(lambda c: c.start())

    def tail_copies(fn):
        @pl.when(i == N_EXPERTS)
        def _():
            def body(blk, _):
                dst = pl.multiple_of(blk * block_rows, block_rows)
                fn(_row_copy(zbuf, xs_ref.at[pl.ds(dst, block_rows)], sem_pad))
                return 0
            lax.fori_loop(n_used, n_blocks, body, 0)

    tail_copies(lambda c: c.start())

    def issue(j, _):
        for k in range(TOP_K):
            dst = _slot_of(pk_ref, ps_ref, _assignment_index(b, t, rows, seq, j, k))
            _row_copy(h2_ref.at[pl.ds(j, 1)], xs_ref.at[pl.ds(dst, 1)], sem).start(priority=k)
        return 0

    lax.fori_loop(0, rows, issue, 0, unroll=8)

    def drain(j, _):
        for k in range(TOP_K):
            _row_copy(h2_ref.at[pl.ds(0, 1)], xs_ref.at[pl.ds(0, 1)], sem).wait()
        return 0

    lax.fori_loop(0, rows, drain, 0, unroll=8)
    pad_copies(lambda c: c.wait())
    tail_copies(lambda c: c.wait())


def _dispatch(pk, pstarts, meta, h2, w3raw, w2raw, n_slots):
    bsz, seq, d = h2.shape
    rows = DISP_ROWS
    assert seq % rows == 0
    per_b = seq // rows
    steps = bsz * per_b
    assert steps > N_EXPERTS
    raws = (w3raw, w2raw)
    assert all(w.shape[0] % (2 * SUBLANES * steps) == 0 for w in raws)
    step = lambda b, t, *_: (b * per_b + t, 0)
    return pl.pallas_call(
        _disp_kernel,
        grid_spec=pltpu.PrefetchScalarGridSpec(
            num_scalar_prefetch=3,
            grid=(bsz, per_b),
            in_specs=[pl.BlockSpec((rows, d), step)]
                     + [pl.BlockSpec((w.shape[0] // steps, w.shape[1]), step) for w in raws],
            out_specs=[pl.BlockSpec(memory_space=pl.ANY)]
                      + [pl.BlockSpec((w.shape[0] // steps // 2, w.shape[1]), step) for w in raws],
            scratch_shapes=[pltpu.VMEM((MOE_ROWS, d), jnp.float32),
                            pltpu.SemaphoreType.DMA,
                            pltpu.SemaphoreType.DMA],
        ),
        out_shape=[jax.ShapeDtypeStruct((n_slots, d), jnp.float32)]
                  + [jax.ShapeDtypeStruct((w.shape[0] // 2, w.shape[1]), jnp.uint32) for w in raws],
        compiler_params=_params(2),
        name="dispatch",
    )(pk, pstarts, meta, h2.reshape(bsz * seq, d), w3raw, w2raw)


def _moe_kernel(block_e_ref, n_used_ref, xs_ref, w1_ref, w3_ref, w2_ref, y_ref):
    del block_e_ref

    @pl.when(pl.program_id(0) < n_used_ref[0])
    def _():
        x = xs_ref[...].astype(jnp.bfloat16)
        h1 = _dot(x, _unpack_rows(w1_ref[0]))
        h3 = _dot(x, _unpack_rows(w3_ref[0]))
        hid = (jax.nn.silu(h1) * h3).astype(jnp.bfloat16)
        y_ref[...] = _dot(hid, _unpack_rows(w2_ref[0]))

    @pl.when(pl.program_id(0) >= n_used_ref[0])
    def _():
        y_ref[...] = jnp.zeros_like(y_ref)


def _moe(block_e, n_used, xs, w1p, w3p, w2p):
    n_slots, d = xs.shape
    f = w1p.shape[2]
    rows = MOE_ROWS
    n_blocks = n_slots // rows
    blk = lambda i, be, nu: (jnp.minimum(i, nu[0] - 1), 0)
    wsel = lambda i, be, nu: (be[i], 0, 0)
    return pl.pallas_call(
        _moe_kernel,
        grid_spec=pltpu.PrefetchScalarGridSpec(
            num_scalar_prefetch=2,
            grid=(n_blocks,),
            in_specs=[pl.BlockSpec((rows, d), blk),
                      pl.BlockSpec((1, d // 2, f), wsel),
                      pl.BlockSpec((1, d // 2, f), wsel),
                      pl.BlockSpec((1, f // 2, d), wsel)],
            out_specs=pl.BlockSpec((rows, d), lambda i, be, nu: (i, 0)),
        ),
        out_shape=jax.ShapeDtypeStruct((n_slots, d), jnp.float32),
        compiler_params=_params(1),
        name="moe",
    )(block_e, n_used, xs, w1p, w3p, w2p)


def _final_kernel(pk_ref, ps_ref, x1_ref, mod_ref, gr_ref, g_fin_ref, yb_ref, o_ref, ybuf, sem):
    b = pl.program_id(0)
    t = pl.program_id(1)
    per_b = pl.num_programs(1)
    g = b * per_b + t
    total = pl.num_programs(0) * per_b
    rows = x1_ref.shape[1]
    d = x1_ref.shape[2]
    seq = rows * per_b

    def issue(step, slot):
        bb = step // per_b
        tt = step - bb * per_b

        def body(j, _):
            for k in range(TOP_K):
                src = _slot_of(pk_ref, ps_ref, _assignment_index(bb, tt, rows, seq, j, k))
                _row_copy(yb_ref.at[pl.ds(src, 1)], ybuf.at[slot, k, pl.ds(j, 1)],
                          sem.at[slot]).start(priority=k)
            return 0

        lax.fori_loop(0, rows, body, 0, unroll=8)

    @pl.when(g == 0)
    def _():
        issue(g, 0)

    slot = g % 2

    @pl.when(g + 1 < total)
    def _():
        issue(g + 1, 1 - slot)

    def drain(j, _):
        for k in range(TOP_K):
            _row_copy(yb_ref.at[pl.ds(0, 1)], ybuf.at[slot, k, pl.ds(0, 1)], sem.at[slot]).wait()
        return 0

    lax.fori_loop(0, rows, drain, 0, unroll=8)

    gt2 = mod_ref[0][:, 5 * d:6 * d]
    gates = gr_ref[0]
    ffn = ybuf[slot, 0] * gates[:, 0:1] + ybuf[slot, 1] * gates[:, 1:2]
    xo = x1_ref[0] + gt2 * ffn
    ms = jnp.mean(xo * xo, axis=-1, keepdims=True)
    o_ref[0] = (xo * lax.rsqrt(ms + EPS) * g_fin_ref[...]).astype(o_ref.dtype)


def _final(pk, pstarts, x1, mod3, gate_rows, g_final, yb, out_dtype):
    bsz, seq, d = x1.shape
    rows = FIN_ROWS
    assert seq % rows == 0
    per_b = seq // rows
    tile = lambda b, t, *_: (b, t, 0)
    return pl.pallas_call(
        _final_kernel,
        grid_spec=pltpu.PrefetchScalarGridSpec(
            num_scalar_prefetch=2,
            grid=(bsz, per_b),
            in_specs=[pl.BlockSpec((1, rows, d), tile),
                      pl.BlockSpec((1, 1, mod3.shape[2]), lambda b, t, *_: (b, 0, 0)),
                      pl.BlockSpec((1, rows, LANES), tile),
                      pl.BlockSpec((1, d), lambda b, t, *_: (0, 0)),
                      pl.BlockSpec(memory_space=pl.ANY)],
            out_specs=pl.BlockSpec((1, rows, d), tile),
            scratch_shapes=[pltpu.VMEM((2, TOP_K, rows, d), jnp.float32),
                            pltpu.SemaphoreType.DMA((2,))],
        ),
        out_shape=jax.ShapeDtypeStruct((bsz, seq, d), out_dtype),
        compiler_params=_params(2),
        name="final",
    )(pk, pstarts, x1, mod3, gate_rows, g_final, yb)


def kernel(x, c, w_ada, b_ada, g_mix, w_in, lru_conv_w, lru_conv_b, lru_w_a, lru_b_a, lru_w_x, lru_b_x,
           lru_lambda, lru_out_g, sc_conv_w, sc_out_g, w_out, g_ffn, w_router_group, b_router_group,
           w_router_expert, b_router_expert, w1, w3, w2, g_final):
    out_dtype = x.dtype
    bsz, seq, d = x.shape
    depth = w_ada.shape[0]
    assert depth == 1, "the final kernel fuses the last layer's residual add with the final norm"
    n_tok = bsz * seq
    n_exp, _, f = w1.shape[1:]
    assert n_exp == N_EXPERTS and n_tok * TOP_K <= RANK_MASK
    row = lambda v: v.reshape(1, -1)
    l = 0

    c_pad = jnp.pad(c.astype(jnp.float32), ((0, 2 * SUBLANES - bsz), (0, 0)))
    n_blocks = -(-n_tok * TOP_K // MOE_ROWS) + n_exp
    n_slots = n_blocks * MOE_ROWS

    mod = _ada(c_pad, w_ada[l], row(b_ada[l]))[:bsz]
    mod3 = mod.reshape(bsz, 1, 6 * d)

    wr = jnp.concatenate([w_router_group[l], jnp.zeros((d, SUBLANES - N_GROUPS), jnp.float32),
                          w_router_expert[l], jnp.zeros((d, LANES - ROUTE_ROWS), jnp.float32)], axis=1)
    wr_hi = wr.astype(jnp.bfloat16)
    wr = jnp.stack([wr_hi, (wr - wr_hi.astype(jnp.float32)).astype(jnp.bfloat16)])
    br = jnp.concatenate([b_router_group[l], jnp.zeros((SUBLANES - N_GROUPS,), jnp.float32),
                          b_router_expert[l], jnp.zeros((LANES - ROUTE_ROWS,), jnp.float32)]).reshape(LANES, 1)
    hd = lru_w_a.shape[-1]
    pack_heads = lambda w: _pack_weight(w.reshape(LRU_HEADS * hd, hd)).reshape(LRU_HEADS, hd // 2, hd)
    x1, h2, pk, gate_rows, cnt, w1p = _mixer(
        x.astype(jnp.float32), mod3, w1[l].reshape(n_exp * d, f), row(g_mix[l]), _pack_weight(w_in[l]),
        lru_conv_w[l], row(lru_conv_b[l]), pack_heads(lru_w_a[l]), row(lru_b_a[l]),
        pack_heads(lru_w_x[l]), row(lru_b_x[l]), row(lru_lambda[l]), row(lru_out_g[l]), sc_conv_w[l],
        row(sc_out_g[l]), _pack_weight(w_out[l]), row(g_ffn[l]), wr, br)

    counts = cnt[:, 0].astype(jnp.int32)
    padded = ((counts + MOE_ROWS - 1) // MOE_ROWS) * MOE_ROWS
    pends = jnp.cumsum(padded)
    pstarts = pends - padded
    n_used = pends[-1] // MOE_ROWS
    blk_start = jnp.minimum(jnp.arange(n_blocks, dtype=jnp.int32), n_used - 1) * MOE_ROWS
    block_e = jnp.minimum(jnp.sum((blk_start[:, None] >= pends[None, :]).astype(jnp.int32), axis=1),
                          n_exp - 1)
    meta = jnp.concatenate([pstarts + counts, padded - counts, n_used[None]]).astype(jnp.int32)
    pk_flat = pk.reshape(n_tok * TOP_K)

    xs, w3p, w2p = _dispatch(pk_flat, pstarts, meta, h2, w3[l].reshape(n_exp * d, f),
                             w2[l].reshape(n_exp * f, d), n_slots)
    yb = _moe(block_e, n_used.reshape(1), xs, w1p.reshape(n_exp, d // 2, f),
              w3p.reshape(n_exp, d // 2, f), w2p.reshape(n_exp, f // 2, d))
    return _final(pk_flat, pstarts, x1, mod3, gate_rows, row(g_final), yb, out_dtype)
```

```python
import jax
import jax.numpy as jnp
from jax import lax
from jax.experimental import pallas as pl
from jax.experimental.pallas import tpu as pltpu

EPS = 1e-6
LRU_C = 8.0
LRU_HEADS = 4
SC_GROUPS = 8
N_GROUPS = 4
EXPERTS_PER_GROUP = 8
N_EXPERTS = N_GROUPS * EXPERTS_PER_GROUP
TOP_K = 2

SUBLANES = 8
LANES = 128
VMEM_LIMIT_BYTES = 60 * 1024 * 1024

ADA_BLOCK_COLS = 1536
PACK_BLOCK_ROWS = 512
MIX_ROWS = 256
DISP_ROWS = 256
MOE_ROWS = 256
FIN_ROWS = 256
ROUTE_ROWS = SUBLANES + N_EXPERTS
RANK_BITS = 20
RANK_MASK = (1 << RANK_BITS) - 1


def _const_spec(shape):
    nd = len(shape)
    return pl.BlockSpec(shape, lambda *_: (0,) * nd, pipeline_mode=pl.Buffered(1))


def _dot(a, b):
    return jnp.dot(a, b, preferred_element_type=jnp.float32)


def _pack_rows(w):
    return pltpu.bitcast(w.astype(jnp.bfloat16), jnp.uint32)


def _unpack_rows(p):
    return pltpu.bitcast(p, jnp.bfloat16)


def _params(n_axes):
    return pltpu.CompilerParams(dimension_semantics=("arbitrary",) * n_axes,
                                vmem_limit_bytes=VMEM_LIMIT_BYTES)


def _pack_kernel(w_ref, o_ref):
    o_ref[...] = _pack_rows(w_ref[...])


def _pack_weight(w2d):
    r, c = w2d.shape
    br = min(PACK_BLOCK_ROWS, r)
    assert r % br == 0
    return pl.pallas_call(
        _pack_kernel,
        grid=(r // br,),
        in_specs=[pl.BlockSpec((br, c), lambda i: (i, 0))],
        out_specs=pl.BlockSpec((br // 2, c), lambda i: (i, 0)),
        out_shape=jax.ShapeDtypeStruct((r // 2, c), jnp.uint32),
        compiler_params=_params(1),
        name="pack",
    )(w2d)


def _ada_kernel(c_ref, w_ref, b_ref, o_ref):
    cond = jax.nn.silu(c_ref[...]).astype(jnp.bfloat16)
    o_ref[...] = _dot(cond, w_ref[...].astype(jnp.bfloat16)) + b_ref[...]


def _ada(c_pad, w_ada, b_ada):
    rows, d = c_pad.shape
    cols = w_ada.shape[1]
    assert cols % ADA_BLOCK_COLS == 0
    return pl.pallas_call(
        _ada_kernel,
        grid=(cols // ADA_BLOCK_COLS,),
        in_specs=[
            pl.BlockSpec((rows, d), lambda j: (0, 0)),
            pl.BlockSpec((d, ADA_BLOCK_COLS), lambda j: (0, j)),
            pl.BlockSpec((1, ADA_BLOCK_COLS), lambda j: (0, j)),
        ],
        out_specs=pl.BlockSpec((rows, ADA_BLOCK_COLS), lambda j: (0, j)),
        out_shape=jax.ShapeDtypeStruct((rows, cols), jnp.float32),
        compiler_params=_params(1),
        name="ada",
    )(c_pad, w_ada, b_ada)


def _shift_rows(x, halo, s):
    if s == 0:
        return x
    rolled = pltpu.roll(x, s, 0)
    row = lax.broadcasted_iota(jnp.int32, (SUBLANES, x.shape[1]), 0)
    top = jnp.where(row < s, pltpu.roll(halo, s, 0), rolled[0:SUBLANES])
    return jnp.concatenate([top, rolled[SUBLANES:]], axis=0)


def _causal_conv(x, halo, w_ref, width):
    y = _shift_rows(x, halo, width - 1) * w_ref[0:1, :]
    for k in range(1, width):
        y = y + _shift_rows(x, halo, width - 1 - k) * w_ref[k:k + 1, :]
    return y


def _group_rms(y, g_ref, n_groups):
    gw = y.shape[1] // n_groups
    outs = []
    for g in range(n_groups):
        yg = y[:, g * gw:(g + 1) * gw]
        ms = jnp.mean(yg * yg, axis=-1, keepdims=True)
        outs.append(yg * lax.rsqrt(ms + EPS))
    return jnp.concatenate(outs, axis=-1) * g_ref[...]


def _linear_scan(a, u, h0):
    t, ch = a.shape
    r8 = lax.broadcasted_iota(jnp.int32, (1, SUBLANES, ch), 1)
    for d in (1, 2, 4):
        keep = r8 >= d
        a_sh = pltpu.roll(a, d, 0)
        u_sh = pltpu.roll(u, d, 0)
        u_new = (a * u_sh + u).reshape(t // SUBLANES, SUBLANES, ch)
        a_new = (a * a_sh).reshape(t // SUBLANES, SUBLANES, ch)
        u = jnp.where(keep, u_new, u.reshape(t // SUBLANES, SUBLANES, ch)).reshape(t, ch)
        a = jnp.where(keep, a_new, a.reshape(t // SUBLANES, SUBLANES, ch)).reshape(t, ch)
    carry = h0
    outs = []
    for j in range(t // SUBLANES):
        hj = u[j * SUBLANES:(j + 1) * SUBLANES] + a[j * SUBLANES:(j + 1) * SUBLANES] * carry
        outs.append(hj)
        carry = hj[SUBLANES - 1:SUBLANES]
    return jnp.concatenate(outs, axis=0), carry


def _first_argmax(v, vmax, n):
    idx = lax.broadcasted_iota(jnp.int32, v.shape, 0)
    return jnp.min(jnp.where(v == vmax, idx, n), axis=0, keepdims=True)


def _mixer_kernel(x_ref, mod_ref, w1raw_ref, w3raw_ref, g_mix_ref, w_in_ref, lcw_ref, lcb_ref, wa_ref, ba_ref, wx_ref,
                  bx_ref, lam_ref, log_ref, scw_ref, sog_ref, w_out_ref, g_ffn_ref, wr_ref, br_ref, tri_ref,
                  x1_ref, pk_ref, gr_ref, cnt_ref, w1p_ref, w3p_ref,
                  lru_halo, sc_halo, h_state, lru_halo_in, sc_halo_in, h_state_in, lt_keep, base):
    b = pl.program_id(0)
    t = pl.program_id(1)
    last = pl.num_programs(1) - 1
    d = x_ref.shape[2]
    rows = x_ref.shape[1]
    lw = lam_ref.shape[1]
    sw = scw_ref.shape[1]
    hd = lw // LRU_HEADS
    cols = (0, lw, 2 * lw, 2 * lw + sw, 2 * lw + 2 * sw, 2 * lw + 3 * sw)

    w1p_ref[...] = _pack_rows(w1raw_ref[...])
    w3p_ref[...] = _pack_rows(w3raw_ref[...])

    @pl.when(t == 0)
    def _():
        lru_halo[...] = jnp.zeros_like(lru_halo)
        sc_halo[...] = jnp.zeros_like(sc_halo)
        h_state[...] = jnp.zeros_like(h_state)
        lru_halo_in[...] = jnp.zeros_like(lru_halo_in)
        sc_halo_in[...] = jnp.zeros_like(sc_halo_in)
        h_state_in[...] = jnp.zeros_like(h_state_in)

    @pl.when((t == 0) & (b == 0))
    def _():
        base[...] = jnp.zeros_like(base)
        lt_keep[...] = jnp.zeros_like(lt_keep)

    mod = mod_ref[0]
    sh1, sc1, gt1 = mod[:, 0:d], mod[:, d:2 * d], mod[:, 2 * d:3 * d]
    sh2, sc2 = mod[:, 3 * d:4 * d], mod[:, 4 * d:5 * d]

    live = (t >= 1).astype(jnp.float32)
    lt = jnp.transpose(lt_keep[...]) + br_ref[...]
    gl = lt[0:N_GROUPS]
    gmax = jnp.max(gl, axis=0, keepdims=True)
    gsum = jnp.sum(jnp.exp(gl - gmax), axis=0, keepdims=True)
    g_p = 1.0 / gsum
    g_idx = _first_argmax(gl, gmax, N_GROUPS)
    el = lt[SUBLANES:SUBLANES + N_EXPERTS]
    sel = jnp.zeros((EXPERTS_PER_GROUP, rows), jnp.float32)
    for g in range(N_GROUPS):
        sel = sel + jnp.where(g_idx == g, el[g * EXPERTS_PER_GROUP:(g + 1) * EXPERTS_PER_GROUP], 0.0)
    emax = jnp.max(sel, axis=0, keepdims=True)
    ee = jnp.exp(sel - emax)
    ep = ee / jnp.sum(ee, axis=0, keepdims=True)
    p1 = jnp.max(ep, axis=0, keepdims=True)
    i1 = _first_argmax(ep, p1, EXPERTS_PER_GROUP)
    eidx = lax.broadcasted_iota(jnp.int32, ep.shape, 0)
    rest = jnp.where(eidx == i1, -1.0, ep)
    p2 = jnp.max(rest, axis=0, keepdims=True)
    i2 = _first_argmax(rest, p2, EXPERTS_PER_GROUP)
    psum = p1 + p2
    e1 = g_idx * EXPERTS_PER_GROUP + i1
    e2 = g_idx * EXPERTS_PER_GROUP + i2

    xidx = lax.broadcasted_iota(jnp.int32, (N_EXPERTS, rows), 0)
    oh1 = xidx == e1
    oh2 = xidx == e2
    cnt = jnp.where(oh1 | oh2, live, 0.0)
    g_lanes = jnp.concatenate([g_p * (p1 / psum), g_p * (p2 / psum),
                               jnp.zeros((LANES - TOP_K, rows), jnp.float32)], axis=0)
    gr_ref[0] = jnp.transpose(g_lanes)

    redo = t == last
    halo_l = jnp.where(redo, lru_halo_in[...], lru_halo[...])
    halo_s = jnp.where(redo, sc_halo_in[...], sc_halo[...])
    h_in = jnp.where(redo, h_state_in[...], h_state[...])
    lru_halo_in[...] = halo_l
    sc_halo_in[...] = halo_s
    h_state_in[...] = h_in

    xt = x_ref[0]
    ms = jnp.mean(xt * xt, axis=-1, keepdims=True)
    h = xt * lax.rsqrt(ms + EPS) * g_mix_ref[...]
    h = (h * (1.0 + sc1) + sh1).astype(jnp.bfloat16)

    def in_proj(i):
        return _dot(h, _unpack_rows(w_in_ref[:, cols[i]:cols[i + 1]]))

    x_lru = in_proj(0)
    xc = _causal_conv(x_lru, halo_l, lcw_ref, lcw_ref.shape[0]) + lcb_ref[...]
    lru_halo[...] = x_lru[rows - SUBLANES:rows]
    xcb = xc.astype(jnp.bfloat16)
    ra = jnp.concatenate([_dot(xcb[:, i * hd:(i + 1) * hd], _unpack_rows(wa_ref[i]))
                          for i in range(LRU_HEADS)], axis=-1)
    rx = jnp.concatenate([_dot(xcb[:, i * hd:(i + 1) * hd], _unpack_rows(wx_ref[i]))
                          for i in range(LRU_HEADS)], axis=-1)
    c_sc = in_proj(3)
    r = jax.nn.sigmoid(ra + ba_ref[...])
    ig = jax.nn.sigmoid(rx + bx_ref[...])
    nl = -lam_ref[...]
    softplus = jnp.maximum(nl, 0.0) + jnp.log1p(jnp.exp(-jnp.abs(nl)))
    log_a = (-LRU_C) * r * softplus
    a = jnp.exp(log_a)
    th = jnp.tanh(log_a)
    mult = jnp.sqrt(-2.0 * th / (1.0 - th))
    x_sc = in_proj(4)
    hl, carry = _linear_scan(a, mult * (ig * xc), h_in)
    h_state[...] = carry
    gate = in_proj(1)
    y_lru = _group_rms(hl * jax.nn.gelu(gate), log_ref, LRU_HEADS)
    mix_lru = _dot(y_lru.astype(jnp.bfloat16), _unpack_rows(w_out_ref[0:lw // 2, :]))

    b_sc = in_proj(2)
    cx = c_sc * x_sc
    y_sc = b_sc * _causal_conv(cx, halo_s, scw_ref, scw_ref.shape[0])
    sc_halo[...] = cx[rows - SUBLANES:rows]
    y_sc = _group_rms(y_sc, sog_ref, SC_GROUPS)

    mix = mix_lru + _dot(y_sc.astype(jnp.bfloat16), _unpack_rows(w_out_ref[lw // 2:(lw + sw) // 2, :]))
    x1 = xt + gt1 * mix
    x1_ref[0] = x1

    ms2 = jnp.mean(x1 * x1, axis=-1, keepdims=True)
    h2 = x1 * lax.rsqrt(ms2 + EPS) * g_ffn_ref[...]
    h2 = h2 * (1.0 + sc2) + sh2
    h2_hi = h2.astype(jnp.bfloat16)
    h2_lo = (h2 - h2_hi.astype(jnp.float32)).astype(jnp.bfloat16)
    lt_keep[...] = (_dot(h2_hi, wr_ref[0]) + _dot(h2_lo, wr_ref[0])) + _dot(h2_hi, wr_ref[1])

    before = _dot(cnt.astype(jnp.bfloat16), tri_ref[...]) + base[:, 0:1]
    rank1 = jnp.sum(jnp.where(oh1, before, 0.0), axis=0, keepdims=True).astype(jnp.int32)
    rank2 = jnp.sum(jnp.where(oh2, before, 0.0), axis=0, keepdims=True).astype(jnp.int32)
    base[...] = base[...] + jnp.sum(cnt, axis=1, keepdims=True)
    cnt_ref[...] = base[...]
    pk_ref[0] = jnp.concatenate([(e1 << RANK_BITS) | rank1, (e2 << RANK_BITS) | rank2], axis=0)


def _mixer(x, mod3, w1raw, w3raw, g_mix, w_in, lcw, lcb, wa, ba, wx, bx, lam, log, scw, sog, w_out, g_ffn, wr, br):
    bsz, seq, d = x.shape
    rows = MIX_ROWS
    assert seq % rows == 0
    per_b = seq // rows
    steps = bsz * per_b
    lw = lam.shape[1]
    sw = scw.shape[1]
    raws = (w1raw, w3raw)
    assert all(w.shape[0] % (2 * SUBLANES * steps) == 0 for w in raws)
    tri = (lax.broadcasted_iota(jnp.int32, (rows, rows), 0)
           < lax.broadcasted_iota(jnp.int32, (rows, rows), 1)).astype(jnp.bfloat16)
    consts = [g_mix, w_in, lcw, lcb, wa, ba, wx, bx, lam, log, scw, sog, w_out, g_ffn, wr, br, tri]
    tile = lambda bb, tt: (bb, jnp.minimum(tt, per_b - 1), 0)
    routed = lambda bb, tt: (bb, jnp.maximum(tt - 1, 0), 0)
    routed_lanes = lambda bb, tt: (bb, 0, jnp.maximum(tt - 1, 0))
    step = lambda bb, tt: (bb * per_b + jnp.minimum(tt, per_b - 1), 0)
    return pl.pallas_call(
        _mixer_kernel,
        grid=(bsz, per_b + 1),
        in_specs=[pl.BlockSpec((1, rows, d), tile),
                  pl.BlockSpec((1, 1, mod3.shape[2]), lambda bb, tt: (bb, 0, 0))]
                 + [pl.BlockSpec((w.shape[0] // steps, w.shape[1]), step) for w in raws]
                 + [_const_spec(c.shape) for c in consts],
        out_specs=[pl.BlockSpec((1, rows, d), tile),
                   pl.BlockSpec((1, TOP_K, rows), routed_lanes),
                   pl.BlockSpec((1, rows, LANES), routed),
                   pl.BlockSpec((N_EXPERTS, LANES), lambda bb, tt: (0, 0))]
                  + [pl.BlockSpec((w.shape[0] // steps // 2, w.shape[1]), step) for w in raws],
        out_shape=[jax.ShapeDtypeStruct((bsz, seq, d), jnp.float32),
                   jax.ShapeDtypeStruct((bsz, TOP_K, seq), jnp.int32),
                   jax.ShapeDtypeStruct((bsz, seq, LANES), jnp.float32),
                   jax.ShapeDtypeStruct((N_EXPERTS, LANES), jnp.float32)]
                  + [jax.ShapeDtypeStruct((w.shape[0] // 2, w.shape[1]), jnp.uint32) for w in raws],
        scratch_shapes=[pltpu.VMEM((SUBLANES, lw), jnp.float32),
                        pltpu.VMEM((SUBLANES, sw), jnp.float32),
                        pltpu.VMEM((1, lw), jnp.float32)] * 2
                       + [pltpu.VMEM((rows, LANES), jnp.float32),
                          pltpu.VMEM((N_EXPERTS, LANES), jnp.float32)],
        compiler_params=_params(2),
        name="mixer",
    )(x, mod3, *raws, *consts)


def _row_copy(src, dst, sem):
    return pltpu.make_async_copy(src, dst, sem)


def _slot_of(pk_ref, ps_ref, idx):
    p = pk_ref[idx]
    return ps_ref[lax.shift_right_logical(p, RANK_BITS)] + (p & RANK_MASK)


def _assignment_index(b, t, rows, seq, j, k):
    return (b * TOP_K + k) * seq + t * rows + j


def _disp_kernel(pk_ref, ps_ref, meta_ref, x1_ref, mod_ref, g_ffn_ref, w2raw_ref, xs_ref, w2p_ref,
                 h2buf, zbuf, sem, sem_pad):
    b = pl.program_id(0)
    t = pl.program_id(1)
    i = b * pl.num_programs(1) + t
    total = pl.num_programs(0) * pl.num_programs(1)
    rows = x1_ref.shape[1]
    d = x1_ref.shape[2]
    seq = rows * pl.num_programs(1)
    block_rows = zbuf.shape[0]
    n_blocks = xs_ref.shape[0] // block_rows
    slot = i % 2

    w2p_ref[...] = _pack_rows(w2raw_ref[...])

    @pl.when(i == 0)
    def _():
        zbuf[...] = jnp.zeros_like(zbuf)

    mod = mod_ref[0]
    sh2, sc2 = mod[:, 3 * d:4 * d], mod[:, 4 * d:5 * d]
    x1 = x1_ref[0]
    ms2 = jnp.mean(x1 * x1, axis=-1, keepdims=True)
    h2 = x1 * lax.rsqrt(ms2 + EPS) * g_ffn_ref[...]
    h2buf[slot] = h2 * (1.0 + sc2) + sh2

    pad_start = meta_ref[jnp.minimum(i, N_EXPERTS - 1)]
    pad_len = meta_ref[N_EXPERTS + jnp.minimum(i, N_EXPERTS - 1)]
    n_used = meta_ref[2 * N_EXPERTS]

    def pad_copies(fn):
        @pl.when(i < N_EXPERTS)
        def _():
            def body(j, _):
                fn(_row_copy(zbuf.at[pl.ds(0, 1)], xs_ref.at[pad_start + j], sem_pad))
                return 0
            lax.fori_loop(0, pad_len, body, 0)

    pad_copies(lambda c: c.start())

    def tail_copies(fn):
        @pl.when(i == N_EXPERTS)
        def _():
            def body(blk, _):
                dst = pl.multiple_of(blk * block_rows, block_rows)
                fn(_row_copy(zbuf, xs_ref.at[pl.ds(dst, block_rows), 0], sem_pad))
                return 0
            lax.fori_loop(n_used, n_blocks, body, 0)

    tail_copies(lambda c: c.start())

    def issue(j, _):
        for k in range(TOP_K):
            dst = _slot_of(pk_ref, ps_ref, _assignment_index(b, t, rows, seq, j, k))
            _row_copy(h2buf.at[slot, pl.ds(j, 1)], xs_ref.at[dst], sem.at[slot]).start(priority=k)
        return 0

    lax.fori_loop(0, rows, issue, 0, unroll=8)

    def drain(s):
        def body(j, _):
            for k in range(TOP_K):
                _row_copy(h2buf.at[s, pl.ds(0, 1)], xs_ref.at[0], sem.at[s]).wait()
            return 0
        lax.fori_loop(0, rows, body, 0, unroll=8)

    @pl.when(i >= 1)
    def _():
        drain(1 - slot)

    @pl.when(i == total - 1)
    def _():
        drain(slot)

    pad_copies(lambda c: c.wait())
    tail_copies(lambda c: c.wait())


def _dispatch(pk, pstarts, meta, x1, mod3, g_ffn, w2raw, n_slots):
    bsz, seq, d = x1.shape
    rows = DISP_ROWS
    assert seq % rows == 0
    per_b = seq // rows
    steps = bsz * per_b
    assert steps > N_EXPERTS
    assert w2raw.shape[0] % (2 * SUBLANES * steps) == 0
    wblk = w2raw.shape[0] // steps
    step = lambda b, t, *_: (b * per_b + t, 0)
    return pl.pallas_call(
        _disp_kernel,
        grid_spec=pltpu.PrefetchScalarGridSpec(
            num_scalar_prefetch=3,
            grid=(bsz, per_b),
            in_specs=[pl.BlockSpec((1, rows, d), lambda b, t, *_: (b, t, 0)),
                      pl.BlockSpec((1, 1, mod3.shape[2]), lambda b, t, *_: (b, 0, 0)),
                      pl.BlockSpec((1, d), lambda b, t, *_: (0, 0)),
                      pl.BlockSpec((wblk, w2raw.shape[1]), step)],
            out_specs=[pl.BlockSpec(memory_space=pl.ANY),
                       pl.BlockSpec((wblk // 2, w2raw.shape[1]), step)],
            scratch_shapes=[pltpu.VMEM((2, rows, d), jnp.float32),
                            pltpu.VMEM((MOE_ROWS, d), jnp.float32),
                            pltpu.SemaphoreType.DMA((2,)),
                            pltpu.SemaphoreType.DMA],
        ),
        out_shape=[jax.ShapeDtypeStruct((n_slots, 1, d), jnp.float32),
                   jax.ShapeDtypeStruct((w2raw.shape[0] // 2, w2raw.shape[1]), jnp.uint32)],
        compiler_params=_params(2),
        name="dispatch",
    )(pk, pstarts, meta, x1, mod3, g_ffn, w2raw)


def _moe_kernel(block_e_ref, n_used_ref, xs_ref, w1_ref, w3_ref, w2_ref, yb_ref, xbuf, ybuf, sem_in, sem_out):
    del block_e_ref
    i = pl.program_id(0)
    n_blocks = pl.num_programs(0)
    n_used = n_used_ref[0]
    rows = xbuf.shape[1]
    slot = i % 2

    def rows_of(ref, blk):
        return ref.at[pl.ds(pl.multiple_of(blk * rows, rows), rows), 0]

    def in_copy(blk, s):
        return pltpu.make_async_copy(rows_of(xs_ref, blk), xbuf.at[s], sem_in.at[s])

    def out_copy(blk, s):
        return pltpu.make_async_copy(ybuf.at[s], rows_of(yb_ref, blk), sem_out.at[s])

    @pl.when(i == 0)
    def _():
        in_copy(0, 0).start()

    @pl.when(i + 1 < n_used)
    def _():
        in_copy(i + 1, 1 - slot).start()

    @pl.when(i >= 2)
    def _():
        out_copy(i - 2, slot).wait()

    @pl.when(i < n_used)
    def _():
        in_copy(i, slot).wait()
        x = xbuf[slot].astype(jnp.bfloat16)
        h1 = _dot(x, _unpack_rows(w1_ref[0]))
        h3 = _dot(x, _unpack_rows(w3_ref[0]))
        hid = (jax.nn.silu(h1) * h3).astype(jnp.bfloat16)
        ybuf[slot] = _dot(hid, _unpack_rows(w2_ref[0]))

    @pl.when(i >= n_used)
    def _():
        ybuf[slot] = jnp.zeros(ybuf.shape[1:], ybuf.dtype)

    out_copy(i, slot).start()

    @pl.when(i == n_blocks - 1)
    def _():
        out_copy(i, slot).wait()

        @pl.when(i >= 1)
        def _():
            out_copy(i - 1, 1 - slot).wait()


def _moe(block_e, n_used, xs, w1p, w3p, w2p):
    n_slots, _, d = xs.shape
    f = w1p.shape[2]
    rows = MOE_ROWS
    n_blocks = n_slots // rows
    wsel = lambda i, be, nu: (be[i], 0, 0)
    return pl.pallas_call(
        _moe_kernel,
        grid_spec=pltpu.PrefetchScalarGridSpec(
            num_scalar_prefetch=2,
            grid=(n_blocks,),
            in_specs=[pl.BlockSpec(memory_space=pl.ANY),
                      pl.BlockSpec((1, d // 2, f), wsel),
                      pl.BlockSpec((1, d // 2, f), wsel),
                      pl.BlockSpec((1, f // 2, d), wsel)],
            out_specs=pl.BlockSpec(memory_space=pl.ANY),
            scratch_shapes=[pltpu.VMEM((2, rows, d), jnp.float32),
                            pltpu.VMEM((2, rows, d), jnp.float32),
                            pltpu.SemaphoreType.DMA((2,)),
                            pltpu.SemaphoreType.DMA((2,))],
        ),
        out_shape=jax.ShapeDtypeStruct((n_slots, 1, d), jnp.float32),
        compiler_params=_params(1),
        name="moe",
    )(block_e, n_used, xs, w1p, w3p, w2p)


def _final_kernel(pk_ref, ps_ref, x1_ref, mod_ref, gr_ref, g_fin_ref, yb_ref, o_ref, ybuf, sem):
    b = pl.program_id(0)
    t = pl.program_id(1)
    per_b = pl.num_programs(1)
    g = b * per_b + t
    total = pl.num_programs(0) * per_b
    rows = x1_ref.shape[1]
    d = x1_ref.shape[2]
    seq = rows * per_b

    def issue(step, slot):
        bb = step // per_b
        tt = step - bb * per_b

        def body(j, _):
            for k in range(TOP_K):
                src = _slot_of(pk_ref, ps_ref, _assignment_index(bb, tt, rows, seq, j, k))
                _row_copy(yb_ref.at[src], ybuf.at[slot, k, pl.ds(j, 1)],
                          sem.at[slot]).start(priority=k)
            return 0

        lax.fori_loop(0, rows, body, 0, unroll=8)

    @pl.when(g == 0)
    def _():
        issue(g, 0)

    slot = g % 2

    @pl.when(g + 1 < total)
    def _():
        issue(g + 1, 1 - slot)

    def drain(j, _):
        for k in range(TOP_K):
            _row_copy(yb_ref.at[0], ybuf.at[slot, k, pl.ds(0, 1)], sem.at[slot]).wait()
        return 0

    lax.fori_loop(0, rows, drain, 0, unroll=8)

    gt2 = mod_ref[0][:, 5 * d:6 * d]
    gates = gr_ref[0]
    ffn = ybuf[slot, 0] * gates[:, 0:1] + ybuf[slot, 1] * gates[:, 1:2]
    xo = x1_ref[0] + gt2 * ffn
    ms = jnp.mean(xo * xo, axis=-1, keepdims=True)
    o_ref[0] = (xo * lax.rsqrt(ms + EPS) * g_fin_ref[...]).astype(o_ref.dtype)


def _final(pk, pstarts, x1, mod3, gate_rows, g_final, yb, out_dtype):
    bsz, seq, d = x1.shape
    rows = FIN_ROWS
    assert seq % rows == 0
    per_b = seq // rows
    tile = lambda b, t, *_: (b, t, 0)
    return pl.pallas_call(
        _final_kernel,
        grid_spec=pltpu.PrefetchScalarGridSpec(
            num_scalar_prefetch=2,
            grid=(bsz, per_b),
            in_specs=[pl.BlockSpec((1, rows, d), tile),
                      pl.BlockSpec((1, 1, mod3.shape[2]), lambda b, t, *_: (b, 0, 0)),
                      pl.BlockSpec((1, rows, LANES), tile),
                      pl.BlockSpec((1, d), lambda b, t, *_: (0, 0)),
                      pl.BlockSpec(memory_space=pl.ANY)],
            out_specs=pl.BlockSpec((1, rows, d), tile),
            scratch_shapes=[pltpu.VMEM((2, TOP_K, rows, d), jnp.float32),
                            pltpu.SemaphoreType.DMA((2,))],
        ),
        out_shape=jax.ShapeDtypeStruct((bsz, seq, d), out_dtype),
        compiler_params=_params(2),
        name="final",
    )(pk, pstarts, x1, mod3, gate_rows, g_final, yb)


def kernel(x, c, w_ada, b_ada, g_mix, w_in, lru_conv_w, lru_conv_b, lru_w_a, lru_b_a, lru_w_x, lru_b_x,
           lru_lambda, lru_out_g, sc_conv_w, sc_out_g, w_out, g_ffn, w_router_group, b_router_group,
           w_router_expert, b_router_expert, w1, w3, w2, g_final):
    out_dtype = x.dtype
    bsz, seq, d = x.shape
    depth = w_ada.shape[0]
    assert depth == 1, "the final kernel fuses the last layer's residual add with the final norm"
    n_tok = bsz * seq
    n_exp, _, f = w1.shape[1:]
    assert n_exp == N_EXPERTS and n_tok * TOP_K <= RANK_MASK
    row = lambda v: v.reshape(1, -1)
    l = 0

    c_pad = jnp.pad(c.astype(jnp.float32), ((0, 2 * SUBLANES - bsz), (0, 0)))
    n_blocks = -(-n_tok * TOP_K // MOE_ROWS) + n_exp
    n_slots = n_blocks * MOE_ROWS

    mod = _ada(c_pad, w_ada[l], row(b_ada[l]))[:bsz]
    mod3 = mod.reshape(bsz, 1, 6 * d)

    wr = jnp.concatenate([w_router_group[l], jnp.zeros((d, SUBLANES - N_GROUPS), jnp.float32),
                          w_router_expert[l], jnp.zeros((d, LANES - ROUTE_ROWS), jnp.float32)], axis=1)
    wr_hi = wr.astype(jnp.bfloat16)
    wr = jnp.stack([wr_hi, (wr - wr_hi.astype(jnp.float32)).astype(jnp.bfloat16)])
    br = jnp.concatenate([b_router_group[l], jnp.zeros((SUBLANES - N_GROUPS,), jnp.float32),
                          b_router_expert[l], jnp.zeros((LANES - ROUTE_ROWS,), jnp.float32)]).reshape(LANES, 1)
    hd = lru_w_a.shape[-1]
    pack_heads = lambda w: _pack_weight(w.reshape(LRU_HEADS * hd, hd)).reshape(LRU_HEADS, hd // 2, hd)
    x1, pk, gate_rows, cnt, w1p, w3p = _mixer(
        x.astype(jnp.float32), mod3, w1[l].reshape(n_exp * d, f), w3[l].reshape(n_exp * d, f),
        row(g_mix[l]), _pack_weight(w_in[l]),
        lru_conv_w[l], row(lru_conv_b[l]), pack_heads(lru_w_a[l]), row(lru_b_a[l]),
        pack_heads(lru_w_x[l]), row(lru_b_x[l]), row(lru_lambda[l]), row(lru_out_g[l]), sc_conv_w[l],
        row(sc_out_g[l]), _pack_weight(w_out[l]), row(g_ffn[l]), wr, br)

    counts = cnt[:, 0].astype(jnp.int32)
    padded = ((counts + MOE_ROWS - 1) // MOE_ROWS) * MOE_ROWS
    pends = jnp.cumsum(padded)
    pstarts = pends - padded
    n_used = pends[-1] // MOE_ROWS
    blk_start = jnp.minimum(jnp.arange(n_blocks, dtype=jnp.int32), n_used - 1) * MOE_ROWS
    block_e = jnp.minimum(jnp.sum((blk_start[:, None] >= pends[None, :]).astype(jnp.int32), axis=1),
                          n_exp - 1)
    meta = jnp.concatenate([pstarts + counts, padded - counts, n_used[None]]).astype(jnp.int32)
    pk_flat = pk.reshape(n_tok * TOP_K)

    xs, w2p = _dispatch(pk_flat, pstarts, meta, x1, mod3, row(g_ffn[l]), w2[l].reshape(n_exp * f, d), n_slots)
    yb = _moe(block_e, n_used.reshape(1), xs, w1p.reshape(n_exp, d // 2, f),
              w3p.reshape(n_exp, d // 2, f), w2p.reshape(n_exp, f // 2, d))
    return _final(pk_flat, pstarts, x1, mod3, gate_rows, row(g_final), yb, out_dtype)
```

```python
import jax
import jax.numpy as jnp
from jax import lax
from jax.experimental import pallas as pl
from jax.experimental.pallas import tpu as pltpu

EPS = 1e-6
LRU_C = 8.0
LRU_HEADS = 4
SC_GROUPS = 8
N_GROUPS = 4
EXPERTS_PER_GROUP = 8
N_EXPERTS = N_GROUPS * EXPERTS_PER_GROUP
TOP_K = 2

SUBLANES = 8
LANES = 128
VMEM_LIMIT_BYTES = 60 * 1024 * 1024

ADA_BLOCK_COLS = 1536
PACK_BLOCK_ROWS = 512
MIX_ROWS = 256
DISP_ROWS = 256
MOE_ROWS = 256
FIN_ROWS = 256
ROUTE_ROWS = SUBLANES + N_EXPERTS
RANK_BITS = 20
RANK_MASK = (1 << RANK_BITS) - 1


def _const_spec(shape):
    nd = len(shape)
    return pl.BlockSpec(shape, lambda *_: (0,) * nd, pipeline_mode=pl.Buffered(1))


def _dot(a, b):
    return jnp.dot(a, b, preferred_element_type=jnp.float32)


def _pack_rows(w):
    return pltpu.bitcast(w.astype(jnp.bfloat16), jnp.uint32)


def _unpack_rows(p):
    return pltpu.bitcast(p, jnp.bfloat16)


def _params(n_axes):
    return pltpu.CompilerParams(dimension_semantics=("arbitrary",) * n_axes,
                                vmem_limit_bytes=VMEM_LIMIT_BYTES)


def _pack_kernel(w_ref, o_ref):
    o_ref[...] = _pack_rows(w_ref[...])


def _pack_weight(w2d):
    r, c = w2d.shape
    br = min(PACK_BLOCK_ROWS, r)
    assert r % br == 0
    return pl.pallas_call(
        _pack_kernel,
        grid=(r // br,),
        in_specs=[pl.BlockSpec((br, c), lambda i: (i, 0))],
        out_specs=pl.BlockSpec((br // 2, c), lambda i: (i, 0)),
        out_shape=jax.ShapeDtypeStruct((r // 2, c), jnp.uint32),
        compiler_params=_params(1),
        name="pack",
    )(w2d)


def _ada_kernel(c_ref, w_ref, b_ref, o_ref):
    cond = jax.nn.silu(c_ref[...]).astype(jnp.bfloat16)
    o_ref[...] = _dot(cond, w_ref[...].astype(jnp.bfloat16)) + b_ref[...]


def _ada(c_pad, w_ada, b_ada):
    rows, d = c_pad.shape
    cols = w_ada.shape[1]
    assert cols % ADA_BLOCK_COLS == 0
    return pl.pallas_call(
        _ada_kernel,
        grid=(cols // ADA_BLOCK_COLS,),
        in_specs=[
            pl.BlockSpec((rows, d), lambda j: (0, 0)),
            pl.BlockSpec((d, ADA_BLOCK_COLS), lambda j: (0, j)),
            pl.BlockSpec((1, ADA_BLOCK_COLS), lambda j: (0, j)),
        ],
        out_specs=pl.BlockSpec((rows, ADA_BLOCK_COLS), lambda j: (0, j)),
        out_shape=jax.ShapeDtypeStruct((rows, cols), jnp.float32),
        compiler_params=_params(1),
        name="ada",
    )(c_pad, w_ada, b_ada)


def _shift_rows(x, halo, s):
    if s == 0:
        return x
    rolled = pltpu.roll(x, s, 0)
    row = lax.broadcasted_iota(jnp.int32, (SUBLANES, x.shape[1]), 0)
    top = jnp.where(row < s, pltpu.roll(halo, s, 0), rolled[0:SUBLANES])
    return jnp.concatenate([top, rolled[SUBLANES:]], axis=0)


def _causal_conv(x, halo, w_ref, width):
    y = _shift_rows(x, halo, width - 1) * w_ref[0:1, :]
    for k in range(1, width):
        y = y + _shift_rows(x, halo, width - 1 - k) * w_ref[k:k + 1, :]
    return y


def _group_rms(y, g_ref, n_groups):
    gw = y.shape[1] // n_groups
    outs = []
    for g in range(n_groups):
        yg = y[:, g * gw:(g + 1) * gw]
        ms = jnp.mean(yg * yg, axis=-1, keepdims=True)
        outs.append(yg * lax.rsqrt(ms + EPS))
    return jnp.concatenate(outs, axis=-1) * g_ref[...]


def _linear_scan(a, u, h0):
    t, ch = a.shape
    r8 = lax.broadcasted_iota(jnp.int32, (1, SUBLANES, ch), 1)
    for d in (1, 2, 4):
        keep = r8 >= d
        a_sh = pltpu.roll(a, d, 0)
        u_sh = pltpu.roll(u, d, 0)
        u_new = (a * u_sh + u).reshape(t // SUBLANES, SUBLANES, ch)
        a_new = (a * a_sh).reshape(t // SUBLANES, SUBLANES, ch)
        u = jnp.where(keep, u_new, u.reshape(t // SUBLANES, SUBLANES, ch)).reshape(t, ch)
        a = jnp.where(keep, a_new, a.reshape(t // SUBLANES, SUBLANES, ch)).reshape(t, ch)
    carry = h0
    outs = []
    for j in range(t // SUBLANES):
        hj = u[j * SUBLANES:(j + 1) * SUBLANES] + a[j * SUBLANES:(j + 1) * SUBLANES] * carry
        outs.append(hj)
        carry = hj[SUBLANES - 1:SUBLANES]
    return jnp.concatenate(outs, axis=0), carry


def _first_argmax(v, vmax, n):
    idx = lax.broadcasted_iota(jnp.int32, v.shape, 0)
    return jnp.min(jnp.where(v == vmax, idx, n), axis=0, keepdims=True)


def _mixer_kernel(x_ref, mod_ref, w1raw_ref, w3raw_ref, g_mix_ref, w_in_ref, lcw_ref, lcb_ref, wa_ref, ba_ref, wx_ref,
                  bx_ref, lam_ref, log_ref, scw_ref, sog_ref, w_out_ref, g_ffn_ref, wr_ref, br_ref, tri_ref,
                  x1_ref, pk_ref, gr_ref, cnt_ref, w1p_ref, w3p_ref,
                  lru_halo, sc_halo, h_state, lru_halo_in, sc_halo_in, h_state_in, lt_keep, base):
    b = pl.program_id(0)
    t = pl.program_id(1)
    last = pl.num_programs(1) - 1
    d = x_ref.shape[2]
    rows = x_ref.shape[1]
    lw = lam_ref.shape[1]
    sw = scw_ref.shape[1]
    hd = lw // LRU_HEADS
    cols = (0, lw, 2 * lw, 2 * lw + sw, 2 * lw + 2 * sw, 2 * lw + 3 * sw)

    w1p_ref[...] = _pack_rows(w1raw_ref[...])
    w3p_ref[...] = _pack_rows(w3raw_ref[...])

    @pl.when(t == 0)
    def _():
        lru_halo[...] = jnp.zeros_like(lru_halo)
        sc_halo[...] = jnp.zeros_like(sc_halo)
        h_state[...] = jnp.zeros_like(h_state)
        lru_halo_in[...] = jnp.zeros_like(lru_halo_in)
        sc_halo_in[...] = jnp.zeros_like(sc_halo_in)
        h_state_in[...] = jnp.zeros_like(h_state_in)

    @pl.when((t == 0) & (b == 0))
    def _():
        base[...] = jnp.zeros_like(base)
        lt_keep[...] = jnp.zeros_like(lt_keep)

    mod = mod_ref[0]
    sh1, sc1, gt1 = mod[:, 0:d], mod[:, d:2 * d], mod[:, 2 * d:3 * d]
    sh2, sc2 = mod[:, 3 * d:4 * d], mod[:, 4 * d:5 * d]

    live = (t >= 1).astype(jnp.float32)
    lt = jnp.transpose(lt_keep[...]) + br_ref[...]
    gl = lt[0:N_GROUPS]
    gmax = jnp.max(gl, axis=0, keepdims=True)
    gsum = jnp.sum(jnp.exp(gl - gmax), axis=0, keepdims=True)
    g_p = 1.0 / gsum
    g_idx = _first_argmax(gl, gmax, N_GROUPS)
    el = lt[SUBLANES:SUBLANES + N_EXPERTS]
    sel = jnp.zeros((EXPERTS_PER_GROUP, rows), jnp.float32)
    for g in range(N_GROUPS):
        sel = sel + jnp.where(g_idx == g, el[g * EXPERTS_PER_GROUP:(g + 1) * EXPERTS_PER_GROUP], 0.0)
    emax = jnp.max(sel, axis=0, keepdims=True)
    ee = jnp.exp(sel - emax)
    ep = ee / jnp.sum(ee, axis=0, keepdims=True)
    p1 = jnp.max(ep, axis=0, keepdims=True)
    i1 = _first_argmax(ep, p1, EXPERTS_PER_GROUP)
    eidx = lax.broadcasted_iota(jnp.int32, ep.shape, 0)
    rest = jnp.where(eidx == i1, -1.0, ep)
    p2 = jnp.max(rest, axis=0, keepdims=True)
    i2 = _first_argmax(rest, p2, EXPERTS_PER_GROUP)
    psum = p1 + p2
    e1 = g_idx * EXPERTS_PER_GROUP + i1
    e2 = g_idx * EXPERTS_PER_GROUP + i2

    xidx = lax.broadcasted_iota(jnp.int32, (N_EXPERTS, rows), 0)
    oh1 = xidx == e1
    oh2 = xidx == e2
    cnt = jnp.where(oh1 | oh2, live, 0.0)
    g_lanes = jnp.concatenate([g_p * (p1 / psum), g_p * (p2 / psum),
                               jnp.zeros((LANES - TOP_K, rows), jnp.float32)], axis=0)
    gr_ref[0] = jnp.transpose(g_lanes)

    redo = t == last
    halo_l = jnp.where(redo, lru_halo_in[...], lru_halo[...])
    halo_s = jnp.where(redo, sc_halo_in[...], sc_halo[...])
    h_in = jnp.where(redo, h_state_in[...], h_state[...])
    lru_halo_in[...] = halo_l
    sc_halo_in[...] = halo_s
    h_state_in[...] = h_in

    xt = x_ref[0]
    ms = jnp.mean(xt * xt, axis=-1, keepdims=True)
    h = xt * lax.rsqrt(ms + EPS) * g_mix_ref[...]
    h = (h * (1.0 + sc1) + sh1).astype(jnp.bfloat16)

    def in_proj(i):
        return _dot(h, _unpack_rows(w_in_ref[:, cols[i]:cols[i + 1]]))

    x_lru = in_proj(0)
    xc = _causal_conv(x_lru, halo_l, lcw_ref, lcw_ref.shape[0]) + lcb_ref[...]
    lru_halo[...] = x_lru[rows - SUBLANES:rows]
    xcb = xc.astype(jnp.bfloat16)
    ra = jnp.concatenate([_dot(xcb[:, i * hd:(i + 1) * hd], _unpack_rows(wa_ref[i]))
                          for i in range(LRU_HEADS)], axis=-1)
    rx = jnp.concatenate([_dot(xcb[:, i * hd:(i + 1) * hd], _unpack_rows(wx_ref[i]))
                          for i in range(LRU_HEADS)], axis=-1)
    c_sc = in_proj(3)
    r = jax.nn.sigmoid(ra + ba_ref[...])
    ig = jax.nn.sigmoid(rx + bx_ref[...])
    nl = -lam_ref[...]
    softplus = jnp.maximum(nl, 0.0) + jnp.log1p(jnp.exp(-jnp.abs(nl)))
    log_a = (-LRU_C) * r * softplus
    a = jnp.exp(log_a)
    th = jnp.tanh(log_a)
    mult = jnp.sqrt(-2.0 * th / (1.0 - th))
    x_sc = in_proj(4)
    hl, carry = _linear_scan(a, mult * (ig * xc), h_in)
    h_state[...] = carry
    gate = in_proj(1)
    y_lru = _group_rms(hl * jax.nn.gelu(gate), log_ref, LRU_HEADS)
    mix_lru = _dot(y_lru.astype(jnp.bfloat16), _unpack_rows(w_out_ref[0:lw // 2, :]))

    b_sc = in_proj(2)
    cx = c_sc * x_sc
    y_sc = b_sc * _causal_conv(cx, halo_s, scw_ref, scw_ref.shape[0])
    sc_halo[...] = cx[rows - SUBLANES:rows]
    y_sc = _group_rms(y_sc, sog_ref, SC_GROUPS)

    mix = mix_lru + _dot(y_sc.astype(jnp.bfloat16), _unpack_rows(w_out_ref[lw // 2:(lw + sw) // 2, :]))
    x1 = xt + gt1 * mix
    x1_ref[0] = x1

    ms2 = jnp.mean(x1 * x1, axis=-1, keepdims=True)
    h2 = x1 * lax.rsqrt(ms2 + EPS) * g_ffn_ref[...]
    h2 = h2 * (1.0 + sc2) + sh2
    h2_hi = h2.astype(jnp.bfloat16)
    h2_lo = (h2 - h2_hi.astype(jnp.float32)).astype(jnp.bfloat16)
    hh = _dot(h2_hi, wr_ref[...])
    lt_keep[...] = (hh[:, 0:LANES] + _dot(h2_lo, wr_ref[:, 0:LANES])) + hh[:, LANES:2 * LANES]

    before = _dot(cnt.astype(jnp.bfloat16), tri_ref[...]) + base[:, 0:1]
    rank1 = jnp.sum(jnp.where(oh1, before, 0.0), axis=0, keepdims=True).astype(jnp.int32)
    rank2 = jnp.sum(jnp.where(oh2, before, 0.0), axis=0, keepdims=True).astype(jnp.int32)
    base[...] = base[...] + jnp.sum(cnt, axis=1, keepdims=True)
    cnt_ref[...] = base[...]
    pk_ref[0] = jnp.concatenate([(e1 << RANK_BITS) | rank1, (e2 << RANK_BITS) | rank2], axis=0)


def _mixer(x, mod3, w1raw, w3raw, g_mix, w_in, lcw, lcb, wa, ba, wx, bx, lam, log, scw, sog, w_out, g_ffn, wr, br):
    bsz, seq, d = x.shape
    rows = MIX_ROWS
    assert seq % rows == 0
    per_b = seq // rows
    steps = bsz * per_b
    lw = lam.shape[1]
    sw = scw.shape[1]
    raws = (w1raw, w3raw)
    assert all(w.shape[0] % (2 * SUBLANES * steps) == 0 for w in raws)
    tri = (lax.broadcasted_iota(jnp.int32, (rows, rows), 0)
           < lax.broadcasted_iota(jnp.int32, (rows, rows), 1)).astype(jnp.bfloat16)
    consts = [g_mix, w_in, lcw, lcb, wa, ba, wx, bx, lam, log, scw, sog, w_out, g_ffn, wr, br, tri]
    tile = lambda bb, tt: (bb, jnp.minimum(tt, per_b - 1), 0)
    routed = lambda bb, tt: (bb, jnp.maximum(tt - 1, 0), 0)
    routed_lanes = lambda bb, tt: (bb, 0, jnp.maximum(tt - 1, 0))
    step = lambda bb, tt: (bb * per_b + jnp.minimum(tt, per_b - 1), 0)
    return pl.pallas_call(
        _mixer_kernel,
        grid=(bsz, per_b + 1),
        in_specs=[pl.BlockSpec((1, rows, d), tile),
                  pl.BlockSpec((1, 1, mod3.shape[2]), lambda bb, tt: (bb, 0, 0))]
                 + [pl.BlockSpec((w.shape[0] // steps, w.shape[1]), step) for w in raws]
                 + [_const_spec(c.shape) for c in consts],
        out_specs=[pl.BlockSpec((1, rows, d), tile),
                   pl.BlockSpec((1, TOP_K, rows), routed_lanes),
                   pl.BlockSpec((1, rows, LANES), routed),
                   pl.BlockSpec((N_EXPERTS, LANES), lambda bb, tt: (0, 0))]
                  + [pl.BlockSpec((w.shape[0] // steps // 2, w.shape[1]), step) for w in raws],
        out_shape=[jax.ShapeDtypeStruct((bsz, seq, d), jnp.float32),
                   jax.ShapeDtypeStruct((bsz, TOP_K, seq), jnp.int32),
                   jax.ShapeDtypeStruct((bsz, seq, LANES), jnp.float32),
                   jax.ShapeDtypeStruct((N_EXPERTS, LANES), jnp.float32)]
                  + [jax.ShapeDtypeStruct((w.shape[0] // 2, w.shape[1]), jnp.uint32) for w in raws],
        scratch_shapes=[pltpu.VMEM((SUBLANES, lw), jnp.float32),
                        pltpu.VMEM((SUBLANES, sw), jnp.float32),
                        pltpu.VMEM((1, lw), jnp.float32)] * 2
                       + [pltpu.VMEM((rows, LANES), jnp.float32),
                          pltpu.VMEM((N_EXPERTS, LANES), jnp.float32)],
        compiler_params=_params(2),
        name="mixer",
    )(x, mod3, *raws, *consts)


def _row_copy(src, dst, sem):
    return pltpu.make_async_copy(src, dst, sem)


def _slot_of(pk_ref, ps_ref, idx):
    p = pk_ref[idx]
    return ps_ref[lax.shift_right_logical(p, RANK_BITS)] + (p & RANK_MASK)


def _assignment_index(b, t, rows, seq, j, k):
    return (b * TOP_K + k) * seq + t * rows + j


def _disp_kernel(pk_ref, ps_ref, meta_ref, x1_ref, mod_ref, g_ffn_ref, xs_ref, h2buf, zbuf, sem, sem_pad):
    b = pl.program_id(0)
    t = pl.program_id(1)
    i = b * pl.num_programs(1) + t
    total = pl.num_programs(0) * pl.num_programs(1)
    rows = x1_ref.shape[1]
    d = x1_ref.shape[2]
    seq = rows * pl.num_programs(1)
    block_rows = zbuf.shape[0]
    n_blocks = xs_ref.shape[0] // block_rows
    slot = i % 2

    @pl.when(i == 0)
    def _():
        zbuf[...] = jnp.zeros_like(zbuf)

    mod = mod_ref[0]
    sh2, sc2 = mod[:, 3 * d:4 * d], mod[:, 4 * d:5 * d]
    x1 = x1_ref[0]
    ms2 = jnp.mean(x1 * x1, axis=-1, keepdims=True)
    h2 = x1 * lax.rsqrt(ms2 + EPS) * g_ffn_ref[...]
    h2buf[slot] = h2 * (1.0 + sc2) + sh2

    pad_start = meta_ref[jnp.minimum(i, N_EXPERTS - 1)]
    pad_len = meta_ref[N_EXPERTS + jnp.minimum(i, N_EXPERTS - 1)]
    n_used = meta_ref[2 * N_EXPERTS]

    def pad_copies(fn):
        @pl.when(i < N_EXPERTS)
        def _():
            def body(j, _):
                fn(_row_copy(zbuf.at[pl.ds(0, 1)], xs_ref.at[pad_start + j], sem_pad))
                return 0
            lax.fori_loop(0, pad_len, body, 0)

    pad_copies(lambda c: c.start())

    def tail_copies(fn):
        @pl.when(i == N_EXPERTS)
        def _():
            def body(blk, _):
                dst = pl.multiple_of(blk * block_rows, block_rows)
                fn(_row_copy(zbuf, xs_ref.at[pl.ds(dst, block_rows), 0], sem_pad))
                return 0
            lax.fori_loop(n_used, n_blocks, body, 0)

    tail_copies(lambda c: c.start())

    def issue(j, _):
        for k in range(TOP_K):
            dst = _slot_of(pk_ref, ps_ref, _assignment_index(b, t, rows, seq, j, k))
            _row_copy(h2buf.at[slot, pl.ds(j, 1)], xs_ref.at[dst], sem.at[slot]).start(priority=k)
        return 0

    lax.fori_loop(0, rows, issue, 0, unroll=8)

    def drain(s):
        def body(j, _):
            for k in range(TOP_K):
                _row_copy(h2buf.at[s, pl.ds(0, 1)], xs_ref.at[0], sem.at[s]).wait()
            return 0
        lax.fori_loop(0, rows, body, 0, unroll=8)

    @pl.when(i >= 1)
    def _():
        drain(1 - slot)

    @pl.when(i == total - 1)
    def _():
        drain(slot)

    pad_copies(lambda c: c.wait())
    tail_copies(lambda c: c.wait())


def _dispatch(pk, pstarts, meta, x1, mod3, g_ffn, n_slots):
    bsz, seq, d = x1.shape
    rows = DISP_ROWS
    assert seq % rows == 0
    per_b = seq // rows
    assert bsz * per_b > N_EXPERTS
    return pl.pallas_call(
        _disp_kernel,
        grid_spec=pltpu.PrefetchScalarGridSpec(
            num_scalar_prefetch=3,
            grid=(bsz, per_b),
            in_specs=[pl.BlockSpec((1, rows, d), lambda b, t, *_: (b, t, 0)),
                      pl.BlockSpec((1, 1, mod3.shape[2]), lambda b, t, *_: (b, 0, 0)),
                      pl.BlockSpec((1, d), lambda b, t, *_: (0, 0))],
            out_specs=pl.BlockSpec(memory_space=pl.ANY),
            scratch_shapes=[pltpu.VMEM((2, rows, d), jnp.float32),
                            pltpu.VMEM((MOE_ROWS, d), jnp.float32),
                            pltpu.SemaphoreType.DMA((2,)),
                            pltpu.SemaphoreType.DMA],
        ),
        out_shape=jax.ShapeDtypeStruct((n_slots, 1, d), jnp.float32),
        compiler_params=_params(2),
        name="dispatch",
    )(pk, pstarts, meta, x1, mod3, g_ffn)


def _moe_kernel(block_e_ref, n_used_ref, xs_ref, w1_ref, w3_ref, w2_ref, yb_ref, xbuf, ybuf, sem_in, sem_out):
    del block_e_ref
    i = pl.program_id(0)
    n_blocks = pl.num_programs(0)
    n_used = n_used_ref[0]
    rows = xbuf.shape[1]
    slot = i % 2

    def rows_of(ref, blk):
        return ref.at[pl.ds(pl.multiple_of(blk * rows, rows), rows), 0]

    def in_copy(blk, s):
        return pltpu.make_async_copy(rows_of(xs_ref, blk), xbuf.at[s], sem_in.at[s])

    def out_copy(blk, s):
        return pltpu.make_async_copy(ybuf.at[s], rows_of(yb_ref, blk), sem_out.at[s])

    @pl.when(i == 0)
    def _():
        in_copy(0, 0).start()

    @pl.when(i + 1 < n_used)
    def _():
        in_copy(i + 1, 1 - slot).start()

    @pl.when(i >= 2)
    def _():
        out_copy(i - 2, slot).wait()

    @pl.when(i < n_used)
    def _():
        in_copy(i, slot).wait()
        x = xbuf[slot].astype(jnp.bfloat16)
        h1 = _dot(x, _unpack_rows(w1_ref[0]))
        h3 = _dot(x, _unpack_rows(w3_ref[0]))
        hid = (jax.nn.silu(h1) * h3).astype(jnp.bfloat16)
        ybuf[slot] = _dot(hid, w2_ref[0].astype(jnp.bfloat16))

    @pl.when(i >= n_used)
    def _():
        ybuf[slot] = jnp.zeros(ybuf.shape[1:], ybuf.dtype)

    out_copy(i, slot).start()

    @pl.when(i == n_blocks - 1)
    def _():
        out_copy(i, slot).wait()

        @pl.when(i >= 1)
        def _():
            out_copy(i - 1, 1 - slot).wait()


def _moe(block_e, n_used, xs, w1p, w3p, w2):
    n_slots, _, d = xs.shape
    f = w1p.shape[2]
    rows = MOE_ROWS
    n_blocks = n_slots // rows
    wsel = lambda i, be, nu: (be[i], 0, 0)
    return pl.pallas_call(
        _moe_kernel,
        grid_spec=pltpu.PrefetchScalarGridSpec(
            num_scalar_prefetch=2,
            grid=(n_blocks,),
            in_specs=[pl.BlockSpec(memory_space=pl.ANY),
                      pl.BlockSpec((1, d // 2, f), wsel),
                      pl.BlockSpec((1, d // 2, f), wsel),
                      pl.BlockSpec((1, f, d), wsel)],
            out_specs=pl.BlockSpec(memory_space=pl.ANY),
            scratch_shapes=[pltpu.VMEM((2, rows, d), jnp.float32),
                            pltpu.VMEM((2, rows, d), jnp.float32),
                            pltpu.SemaphoreType.DMA((2,)),
                            pltpu.SemaphoreType.DMA((2,))],
        ),
        out_shape=jax.ShapeDtypeStruct((n_slots, 1, d), jnp.float32),
        compiler_params=_params(1),
        name="moe",
    )(block_e, n_used, xs, w1p, w3p, w2)


def _final_kernel(pk_ref, ps_ref, x1_ref, mod_ref, gr_ref, g_fin_ref, yb_ref, o_ref, ybuf, sem):
    b = pl.program_id(0)
    t = pl.program_id(1)
    per_b = pl.num_programs(1)
    g = b * per_b + t
    total = pl.num_programs(0) * per_b
    rows = x1_ref.shape[1]
    d = x1_ref.shape[2]
    seq = rows * per_b

    def issue(step, slot):
        bb = step // per_b
        tt = step - bb * per_b

        def body(j, _):
            for k in range(TOP_K):
                src = _slot_of(pk_ref, ps_ref, _assignment_index(bb, tt, rows, seq, j, k))
                _row_copy(yb_ref.at[src], ybuf.at[slot, k, pl.ds(j, 1)],
                          sem.at[slot]).start(priority=k)
            return 0

        lax.fori_loop(0, rows, body, 0, unroll=8)

    @pl.when(g == 0)
    def _():
        issue(g, 0)

    slot = g % 2

    @pl.when(g + 1 < total)
    def _():
        issue(g + 1, 1 - slot)

    def drain(j, _):
        for k in range(TOP_K):
            _row_copy(yb_ref.at[0], ybuf.at[slot, k, pl.ds(0, 1)], sem.at[slot]).wait()
        return 0

    lax.fori_loop(0, rows, drain, 0, unroll=8)

    gt2 = mod_ref[0][:, 5 * d:6 * d]
    gates = gr_ref[0]
    ffn = ybuf[slot, 0] * gates[:, 0:1] + ybuf[slot, 1] * gates[:, 1:2]
    xo = x1_ref[0] + gt2 * ffn
    ms = jnp.mean(xo * xo, axis=-1, keepdims=True)
    o_ref[0] = (xo * lax.rsqrt(ms + EPS) * g_fin_ref[...]).astype(o_ref.dtype)


def _final(pk, pstarts, x1, mod3, gate_rows, g_final, yb, out_dtype):
    bsz, seq, d = x1.shape
    rows = FIN_ROWS
    assert seq % rows == 0
    per_b = seq // rows
    tile = lambda b, t, *_: (b, t, 0)
    return pl.pallas_call(
        _final_kernel,
        grid_spec=pltpu.PrefetchScalarGridSpec(
            num_scalar_prefetch=2,
            grid=(bsz, per_b),
            in_specs=[pl.BlockSpec((1, rows, d), tile),
                      pl.BlockSpec((1, 1, mod3.shape[2]), lambda b, t, *_: (b, 0, 0)),
                      pl.BlockSpec((1, rows, LANES), tile),
                      pl.BlockSpec((1, d), lambda b, t, *_: (0, 0)),
                      pl.BlockSpec(memory_space=pl.ANY)],
            out_specs=pl.BlockSpec((1, rows, d), tile),
            scratch_shapes=[pltpu.VMEM((2, TOP_K, rows, d), jnp.float32),
                            pltpu.SemaphoreType.DMA((2,))],
        ),
        out_shape=jax.ShapeDtypeStruct((bsz, seq, d), out_dtype),
        compiler_params=_params(2),
        name="final",
    )(pk, pstarts, x1, mod3, gate_rows, g_final, yb)


def kernel(x, c, w_ada, b_ada, g_mix, w_in, lru_conv_w, lru_conv_b, lru_w_a, lru_b_a, lru_w_x, lru_b_x,
           lru_lambda, lru_out_g, sc_conv_w, sc_out_g, w_out, g_ffn, w_router_group, b_router_group,
           w_router_expert, b_router_expert, w1, w3, w2, g_final):
    out_dtype = x.dtype
    bsz, seq, d = x.shape
    depth = w_ada.shape[0]
    assert depth == 1, "the final kernel fuses the last layer's residual add with the final norm"
    n_tok = bsz * seq
    n_exp, _, f = w1.shape[1:]
    assert n_exp == N_EXPERTS and n_tok * TOP_K <= RANK_MASK
    row = lambda v: v.reshape(1, -1)
    l = 0

    c_pad = jnp.pad(c.astype(jnp.float32), ((0, 2 * SUBLANES - bsz), (0, 0)))
    n_blocks = -(-n_tok * TOP_K // MOE_ROWS) + n_exp
    n_slots = n_blocks * MOE_ROWS

    mod = _ada(c_pad, w_ada[l], row(b_ada[l]))[:bsz]
    mod3 = mod.reshape(bsz, 1, 6 * d)

    wr = jnp.concatenate([w_router_group[l], jnp.zeros((d, SUBLANES - N_GROUPS), jnp.float32),
                          w_router_expert[l], jnp.zeros((d, LANES - ROUTE_ROWS), jnp.float32)], axis=1)
    wr_hi = wr.astype(jnp.bfloat16)
    wr = jnp.concatenate([wr_hi, (wr - wr_hi.astype(jnp.float32)).astype(jnp.bfloat16)], axis=1)
    br = jnp.concatenate([b_router_group[l], jnp.zeros((SUBLANES - N_GROUPS,), jnp.float32),
                          b_router_expert[l], jnp.zeros((LANES - ROUTE_ROWS,), jnp.float32)]).reshape(LANES, 1)
    hd = lru_w_a.shape[-1]
    pack_heads = lambda w: _pack_weight(w.reshape(LRU_HEADS * hd, hd)).reshape(LRU_HEADS, hd // 2, hd)
    x1, pk, gate_rows, cnt, w1p, w3p = _mixer(
        x.astype(jnp.float32), mod3, w1[l].reshape(n_exp * d, f), w3[l].reshape(n_exp * d, f),
        row(g_mix[l]), _pack_weight(w_in[l]),
        lru_conv_w[l], row(lru_conv_b[l]), pack_heads(lru_w_a[l]), row(lru_b_a[l]),
        pack_heads(lru_w_x[l]), row(lru_b_x[l]), row(lru_lambda[l]), row(lru_out_g[l]), sc_conv_w[l],
        row(sc_out_g[l]), _pack_weight(w_out[l]), row(g_ffn[l]), wr, br)

    counts = cnt[:, 0].astype(jnp.int32)
    padded = ((counts + MOE_ROWS - 1) // MOE_ROWS) * MOE_ROWS
    pends = jnp.cumsum(padded)
    pstarts = pends - padded
    n_used = pends[-1] // MOE_ROWS
    blk_start = jnp.minimum(jnp.arange(n_blocks, dtype=jnp.int32), n_used - 1) * MOE_ROWS
    block_e = jnp.minimum(jnp.sum((blk_start[:, None] >= pends[None, :]).astype(jnp.int32), axis=1),
                          n_exp - 1)
    meta = jnp.concatenate([pstarts + counts, padded - counts, n_used[None]]).astype(jnp.int32)
    pk_flat = pk.reshape(n_tok * TOP_K)

    xs = _dispatch(pk_flat, pstarts, meta, x1, mod3, row(g_ffn[l]), n_slots)
    yb = _moe(block_e, n_used.reshape(1), xs, w1p.reshape(n_exp, d // 2, f),
              w3p.reshape(n_exp, d // 2, f), w2[l])
    return _final(pk_flat, pstarts, x1, mod3, gate_rows, row(g_final), yb, out_dtype)
```

```python
import jax
import jax.numpy as jnp
from jax import lax
from jax.experimental import pallas as pl
from jax.experimental.pallas import tpu as pltpu

EPS = 1e-6
LRU_C = 8.0
LRU_HEADS = 4
SC_GROUPS = 8
N_GROUPS = 4
EXPERTS_PER_GROUP = 8
N_EXPERTS = N_GROUPS * EXPERTS_PER_GROUP
TOP_K = 2

SUBLANES = 8
LANES = 128
VMEM_LIMIT_BYTES = 60 * 1024 * 1024

ADA_BLOCK_COLS = 1536
PACK_BLOCK_ROWS = 512
MIX_ROWS = 256
DISP_ROWS = 256
MOE_ROWS = 256
FIN_ROWS = 256
ROUTE_ROWS = SUBLANES + N_EXPERTS
RANK_BITS = 20
RANK_MASK = (1 << RANK_BITS) - 1


def _const_spec(shape):
    nd = len(shape)
    return pl.BlockSpec(shape, lambda *_: (0,) * nd, pipeline_mode=pl.Buffered(1))


def _dot(a, b):
    return jnp.dot(a, b, preferred_element_type=jnp.float32)


def _pack_rows(w):
    return pltpu.bitcast(w.astype(jnp.bfloat16), jnp.uint32)


def _unpack_rows(p):
    return pltpu.bitcast(p, jnp.bfloat16)


def _params(n_axes):
    return pltpu.CompilerParams(dimension_semantics=("arbitrary",) * n_axes,
                                vmem_limit_bytes=VMEM_LIMIT_BYTES)


def _pack_kernel(w_ref, o_ref):
    o_ref[...] = _pack_rows(w_ref[...])


def _pack_weight(w2d):
    r, c = w2d.shape
    br = min(PACK_BLOCK_ROWS, r)
    assert r % br == 0
    return pl.pallas_call(
        _pack_kernel,
        grid=(r // br,),
        in_specs=[pl.BlockSpec((br, c), lambda i: (i, 0))],
        out_specs=pl.BlockSpec((br // 2, c), lambda i: (i, 0)),
        out_shape=jax.ShapeDtypeStruct((r // 2, c), jnp.uint32),
        compiler_params=_params(1),
        name="pack",
    )(w2d)


def _ada_kernel(c_ref, w_ref, b_ref, o_ref):
    cond = jax.nn.silu(c_ref[...]).astype(jnp.bfloat16)
    o_ref[...] = _dot(cond, w_ref[...].astype(jnp.bfloat16)) + b_ref[...]


def _ada(c_pad, w_ada, b_ada):
    rows, d = c_pad.shape
    cols = w_ada.shape[1]
    assert cols % ADA_BLOCK_COLS == 0
    return pl.pallas_call(
        _ada_kernel,
        grid=(cols // ADA_BLOCK_COLS,),
        in_specs=[
            pl.BlockSpec((rows, d), lambda j: (0, 0)),
            pl.BlockSpec((d, ADA_BLOCK_COLS), lambda j: (0, j)),
            pl.BlockSpec((1, ADA_BLOCK_COLS), lambda j: (0, j)),
        ],
        out_specs=pl.BlockSpec((rows, ADA_BLOCK_COLS), lambda j: (0, j)),
        out_shape=jax.ShapeDtypeStruct((rows, cols), jnp.float32),
        compiler_params=_params(1),
        name="ada",
    )(c_pad, w_ada, b_ada)


def _shift_rows(x, halo, s):
    if s == 0:
        return x
    rolled = pltpu.roll(x, s, 0)
    row = lax.broadcasted_iota(jnp.int32, (SUBLANES, x.shape[1]), 0)
    top = jnp.where(row < s, pltpu.roll(halo, s, 0), rolled[0:SUBLANES])
    return jnp.concatenate([top, rolled[SUBLANES:]], axis=0)


def _causal_conv(x, halo, w_ref, width):
    y = _shift_rows(x, halo, width - 1) * w_ref[0:1, :]
    for k in range(1, width):
        y = y + _shift_rows(x, halo, width - 1 - k) * w_ref[k:k + 1, :]
    return y


def _group_rms(y, g_ref, n_groups):
    gw = y.shape[1] // n_groups
    outs = []
    for g in range(n_groups):
        yg = y[:, g * gw:(g + 1) * gw]
        ms = jnp.mean(yg * yg, axis=-1, keepdims=True)
        outs.append(yg * lax.rsqrt(ms + EPS))
    return jnp.concatenate(outs, axis=-1) * g_ref[...]


def _linear_scan(a, u, h0):
    t, ch = a.shape
    r8 = lax.broadcasted_iota(jnp.int32, (1, SUBLANES, ch), 1)
    for d in (1, 2, 4):
        keep = r8 >= d
        a_sh = pltpu.roll(a, d, 0)
        u_sh = pltpu.roll(u, d, 0)
        u_new = (a * u_sh + u).reshape(t // SUBLANES, SUBLANES, ch)
        a_new = (a * a_sh).reshape(t // SUBLANES, SUBLANES, ch)
        u = jnp.where(keep, u_new, u.reshape(t // SUBLANES, SUBLANES, ch)).reshape(t, ch)
        a = jnp.where(keep, a_new, a.reshape(t // SUBLANES, SUBLANES, ch)).reshape(t, ch)
    carry = h0
    outs = []
    for j in range(t // SUBLANES):
        hj = u[j * SUBLANES:(j + 1) * SUBLANES] + a[j * SUBLANES:(j + 1) * SUBLANES] * carry
        outs.append(hj)
        carry = hj[SUBLANES - 1:SUBLANES]
    return jnp.concatenate(outs, axis=0), carry


def _first_argmax(v, vmax, n):
    idx = lax.broadcasted_iota(jnp.int32, v.shape, 0)
    return jnp.min(jnp.where(v == vmax, idx, n), axis=0, keepdims=True)


def _mixer_kernel(x_ref, mod_ref, w1raw_ref, w3raw_ref, g_mix_ref, w_in_ref, lcw_ref, lcb_ref, wa_ref, ba_ref, wx_ref,
                  bx_ref, lam_ref, log_ref, scw_ref, sog_ref, w_out_ref, g_ffn_ref, wr_ref, br_ref, tri_ref,
                  x1_ref, pk_ref, gr_ref, cnt_ref, w1p_ref, w3p_ref,
                  lru_halo, sc_halo, h_state, lru_halo_in, sc_halo_in, h_state_in, lt_keep, base):
    b = pl.program_id(0)
    t = pl.program_id(1)
    last = pl.num_programs(1) - 1
    d = x_ref.shape[2]
    rows = x_ref.shape[1]
    lw = lam_ref.shape[1]
    sw = scw_ref.shape[1]
    hd = lw // LRU_HEADS
    cols = (0, lw, 2 * lw, 2 * lw + sw, 2 * lw + 2 * sw, 2 * lw + 3 * sw)

    w1p_ref[...] = _pack_rows(w1raw_ref[...])
    w3p_ref[...] = _pack_rows(w3raw_ref[...])

    @pl.when(t == 0)
    def _():
        lru_halo[...] = jnp.zeros_like(lru_halo)
        sc_halo[...] = jnp.zeros_like(sc_halo)
        h_state[...] = jnp.zeros_like(h_state)
        lru_halo_in[...] = jnp.zeros_like(lru_halo_in)
        sc_halo_in[...] = jnp.zeros_like(sc_halo_in)
        h_state_in[...] = jnp.zeros_like(h_state_in)

    @pl.when((t == 0) & (b == 0))
    def _():
        base[...] = jnp.zeros_like(base)
        lt_keep[...] = jnp.zeros_like(lt_keep)

    mod = mod_ref[0]
    sh1, sc1, gt1 = mod[:, 0:d], mod[:, d:2 * d], mod[:, 2 * d:3 * d]
    sh2, sc2 = mod[:, 3 * d:4 * d], mod[:, 4 * d:5 * d]

    live = (t >= 1).astype(jnp.float32)
    lt = jnp.transpose(lt_keep[...]) + br_ref[...]
    gl = lt[0:N_GROUPS]
    gmax = jnp.max(gl, axis=0, keepdims=True)
    gsum = jnp.sum(jnp.exp(gl - gmax), axis=0, keepdims=True)
    g_p = 1.0 / gsum
    g_idx = _first_argmax(gl, gmax, N_GROUPS)
    el = lt[SUBLANES:SUBLANES + N_EXPERTS]
    sel = jnp.zeros((EXPERTS_PER_GROUP, rows), jnp.float32)
    for g in range(N_GROUPS):
        sel = sel + jnp.where(g_idx == g, el[g * EXPERTS_PER_GROUP:(g + 1) * EXPERTS_PER_GROUP], 0.0)
    emax = jnp.max(sel, axis=0, keepdims=True)
    ee = jnp.exp(sel - emax)
    ep = ee / jnp.sum(ee, axis=0, keepdims=True)
    p1 = jnp.max(ep, axis=0, keepdims=True)
    i1 = _first_argmax(ep, p1, EXPERTS_PER_GROUP)
    eidx = lax.broadcasted_iota(jnp.int32, ep.shape, 0)
    rest = jnp.where(eidx == i1, -1.0, ep)
    p2 = jnp.max(rest, axis=0, keepdims=True)
    i2 = _first_argmax(rest, p2, EXPERTS_PER_GROUP)
    psum = p1 + p2
    e1 = g_idx * EXPERTS_PER_GROUP + i1
    e2 = g_idx * EXPERTS_PER_GROUP + i2

    xidx = lax.broadcasted_iota(jnp.int32, (N_EXPERTS, rows), 0)
    oh1 = xidx == e1
    oh2 = xidx == e2
    cnt = jnp.where(oh1 | oh2, live, 0.0)
    g_lanes = jnp.concatenate([g_p * (p1 / psum), g_p * (p2 / psum),
                               jnp.zeros((LANES - TOP_K, rows), jnp.float32)], axis=0)
    gr_ref[0] = jnp.transpose(g_lanes)

    redo = t == last
    halo_l = jnp.where(redo, lru_halo_in[...], lru_halo[...])
    halo_s = jnp.where(redo, sc_halo_in[...], sc_halo[...])
    h_in = jnp.where(redo, h_state_in[...], h_state[...])
    lru_halo_in[...] = halo_l
    sc_halo_in[...] = halo_s
    h_state_in[...] = h_in

    xt = x_ref[0]
    ms = jnp.mean(xt * xt, axis=-1, keepdims=True)
    h = xt * lax.rsqrt(ms + EPS) * g_mix_ref[...]
    h = (h * (1.0 + sc1) + sh1).astype(jnp.bfloat16)

    def in_proj(i):
        return _dot(h, _unpack_rows(w_in_ref[:, cols[i]:cols[i + 1]]))

    x_lru = in_proj(0)
    xc = _causal_conv(x_lru, halo_l, lcw_ref, lcw_ref.shape[0]) + lcb_ref[...]
    lru_halo[...] = x_lru[rows - SUBLANES:rows]
    xcb = xc.astype(jnp.bfloat16)
    ra = jnp.concatenate([_dot(xcb[:, i * hd:(i + 1) * hd], _unpack_rows(wa_ref[i]))
                          for i in range(LRU_HEADS)], axis=-1)
    rx = jnp.concatenate([_dot(xcb[:, i * hd:(i + 1) * hd], _unpack_rows(wx_ref[i]))
                          for i in range(LRU_HEADS)], axis=-1)
    c_sc = in_proj(3)
    r = jax.nn.sigmoid(ra + ba_ref[...])
    ig = jax.nn.sigmoid(rx + bx_ref[...])
    nl = -lam_ref[...]
    softplus = jnp.maximum(nl, 0.0) + jnp.log1p(jnp.exp(-jnp.abs(nl)))
    log_a = (-LRU_C) * r * softplus
    a = jnp.exp(log_a)
    th = jnp.tanh(log_a)
    mult = jnp.sqrt(-2.0 * th / (1.0 - th))
    x_sc = in_proj(4)
    hl, carry = _linear_scan(a, mult * (ig * xc), h_in)
    h_state[...] = carry
    gate = in_proj(1)
    y_lru = _group_rms(hl * jax.nn.gelu(gate), log_ref, LRU_HEADS)
    mix_lru = _dot(y_lru.astype(jnp.bfloat16), _unpack_rows(w_out_ref[0:lw // 2, :]))

    b_sc = in_proj(2)
    cx = c_sc * x_sc
    y_sc = b_sc * _causal_conv(cx, halo_s, scw_ref, scw_ref.shape[0])
    sc_halo[...] = cx[rows - SUBLANES:rows]
    y_sc = _group_rms(y_sc, sog_ref, SC_GROUPS)

    mix = mix_lru + _dot(y_sc.astype(jnp.bfloat16), _unpack_rows(w_out_ref[lw // 2:(lw + sw) // 2, :]))
    x1 = xt + gt1 * mix
    x1_ref[0] = x1

    ms2 = jnp.mean(x1 * x1, axis=-1, keepdims=True)
    h2 = x1 * lax.rsqrt(ms2 + EPS) * g_ffn_ref[...]
    h2 = h2 * (1.0 + sc2) + sh2
    h2_hi = h2.astype(jnp.bfloat16)
    h2_lo = (h2 - h2_hi.astype(jnp.float32)).astype(jnp.bfloat16)
    hh = _dot(h2_hi, wr_ref[...])
    lt_keep[...] = (hh[:, 0:LANES] + _dot(h2_lo, wr_ref[:, 0:LANES])) + hh[:, LANES:2 * LANES]

    before = _dot(cnt.astype(jnp.bfloat16), tri_ref[...]) + base[:, 0:1]
    rank1 = jnp.sum(jnp.where(oh1, before, 0.0), axis=0, keepdims=True).astype(jnp.int32)
    rank2 = jnp.sum(jnp.where(oh2, before, 0.0), axis=0, keepdims=True).astype(jnp.int32)
    base[...] = base[...] + jnp.sum(cnt, axis=1, keepdims=True)
    cnt_ref[...] = base[...]
    pk_ref[0] = jnp.concatenate([(e1 << RANK_BITS) | rank1, (e2 << RANK_BITS) | rank2], axis=0)


def _mixer(x, mod3, w1raw, w3raw, g_mix, w_in, lcw, lcb, wa, ba, wx, bx, lam, log, scw, sog, w_out, g_ffn, wr, br):
    bsz, seq, d = x.shape
    rows = MIX_ROWS
    assert seq % rows == 0
    per_b = seq // rows
    steps = bsz * per_b
    lw = lam.shape[1]
    sw = scw.shape[1]
    raws = (w1raw, w3raw)
    assert all(w.shape[0] % (2 * SUBLANES * steps) == 0 for w in raws)
    tri = (lax.broadcasted_iota(jnp.int32, (rows, rows), 0)
           < lax.broadcasted_iota(jnp.int32, (rows, rows), 1)).astype(jnp.bfloat16)
    consts = [g_mix, w_in, lcw, lcb, wa, ba, wx, bx, lam, log, scw, sog, w_out, g_ffn, wr, br, tri]
    tile = lambda bb, tt: (bb, jnp.minimum(tt, per_b - 1), 0)
    routed = lambda bb, tt: (bb, jnp.maximum(tt - 1, 0), 0)
    routed_lanes = lambda bb, tt: (bb, 0, jnp.maximum(tt - 1, 0))
    step = lambda bb, tt: (bb * per_b + jnp.minimum(tt, per_b - 1), 0)
    return pl.pallas_call(
        _mixer_kernel,
        grid=(bsz, per_b + 1),
        in_specs=[pl.BlockSpec((1, rows, d), tile),
                  pl.BlockSpec((1, 1, mod3.shape[2]), lambda bb, tt: (bb, 0, 0))]
                 + [pl.BlockSpec((w.shape[0] // steps, w.shape[1]), step) for w in raws]
                 + [_const_spec(c.shape) for c in consts],
        out_specs=[pl.BlockSpec((1, rows, d), tile),
                   pl.BlockSpec((1, TOP_K, rows), routed_lanes),
                   pl.BlockSpec((1, rows, LANES), routed),
                   pl.BlockSpec((N_EXPERTS, LANES), lambda bb, tt: (0, 0))]
                  + [pl.BlockSpec((w.shape[0] // steps // 2, w.shape[1]), step) for w in raws],
        out_shape=[jax.ShapeDtypeStruct((bsz, seq, d), jnp.float32),
                   jax.ShapeDtypeStruct((bsz, TOP_K, seq), jnp.int32),
                   jax.ShapeDtypeStruct((bsz, seq, LANES), jnp.float32),
                   jax.ShapeDtypeStruct((N_EXPERTS, LANES), jnp.float32)]
                  + [jax.ShapeDtypeStruct((w.shape[0] // 2, w.shape[1]), jnp.uint32) for w in raws],
        scratch_shapes=[pltpu.VMEM((SUBLANES, lw), jnp.float32),
                        pltpu.VMEM((SUBLANES, sw), jnp.float32),
                        pltpu.VMEM((1, lw), jnp.float32)] * 2
                       + [pltpu.VMEM((rows, LANES), jnp.float32),
                          pltpu.VMEM((N_EXPERTS, LANES), jnp.float32)],
        compiler_params=_params(2),
        name="mixer",
    )(x, mod3, *raws, *consts)


def _row_copy(src, dst, sem):
    return pltpu.make_async_copy(src, dst, sem)


def _slot_of(pk_ref, ps_ref, idx):
    p = pk_ref[idx]
    return ps_ref[lax.shift_right_logical(p, RANK_BITS)] + (p & RANK_MASK)


def _assignment_index(b, t, rows, seq, j, k):
    return (b * TOP_K + k) * seq + t * rows + j


def _disp_kernel(pk_ref, ps_ref, meta_ref, x1_ref, mod_ref, g_ffn_ref, xs_ref, h2buf, zbuf, sem, sem_pad):
    b = pl.program_id(0)
    t = pl.program_id(1)
    i = b * pl.num_programs(1) + t
    total = pl.num_programs(0) * pl.num_programs(1)
    rows = x1_ref.shape[1]
    d = x1_ref.shape[2]
    seq = rows * pl.num_programs(1)
    block_rows = zbuf.shape[0]
    n_blocks = xs_ref.shape[0] // block_rows
    slot = i % 2

    @pl.when(i == 0)
    def _():
        zbuf[...] = jnp.zeros_like(zbuf)

    mod = mod_ref[0]
    sh2, sc2 = mod[:, 3 * d:4 * d], mod[:, 4 * d:5 * d]
    x1 = x1_ref[0]
    ms2 = jnp.mean(x1 * x1, axis=-1, keepdims=True)
    h2 = x1 * lax.rsqrt(ms2 + EPS) * g_ffn_ref[...]
    h2buf[slot] = h2 * (1.0 + sc2) + sh2

    pad_start = meta_ref[jnp.minimum(i, N_EXPERTS - 1)]
    pad_len = meta_ref[N_EXPERTS + jnp.minimum(i, N_EXPERTS - 1)]
    n_used = meta_ref[2 * N_EXPERTS]

    def pad_copies(fn):
        @pl.when(i < N_EXPERTS)
        def _():
            def body(j, _):
                fn(_row_copy(zbuf.at[pl.ds(0, 1)], xs_ref.at[pad_start + j], sem_pad))
                return 0
            lax.fori_loop(0, pad_len, body, 0)

    pad_copies(lambda c: c.start())

    def tail_copies(fn):
        @pl.when(i == N_EXPERTS)
        def _():
            def body(blk, _):
                dst = pl.multiple_of(blk * block_rows, block_rows)
                fn(_row_copy(zbuf, xs_ref.at[pl.ds(dst, block_rows), 0], sem_pad))
                return 0
            lax.fori_loop(n_used, n_blocks, body, 0)

    tail_copies(lambda c: c.start())

    def issue(j, _):
        for k in range(TOP_K):
            dst = _slot_of(pk_ref, ps_ref, _assignment_index(b, t, rows, seq, j, k))
            _row_copy(h2buf.at[slot, pl.ds(j, 1)], xs_ref.at[dst], sem.at[slot]).start(priority=k)
        return 0

    lax.fori_loop(0, rows, issue, 0, unroll=8)

    def drain(s):
        def body(j, _):
            for k in range(TOP_K):
                _row_copy(h2buf.at[s, pl.ds(0, 1)], xs_ref.at[0], sem.at[s]).wait()
            return 0
        lax.fori_loop(0, rows, body, 0, unroll=8)

    @pl.when(i >= 1)
    def _():
        drain(1 - slot)

    @pl.when(i == total - 1)
    def _():
        drain(slot)

    pad_copies(lambda c: c.wait())
    tail_copies(lambda c: c.wait())


def _dispatch(pk, pstarts, meta, x1, mod3, g_ffn, n_slots):
    bsz, seq, d = x1.shape
    rows = DISP_ROWS
    assert seq % rows == 0
    per_b = seq // rows
    assert bsz * per_b > N_EXPERTS
    return pl.pallas_call(
        _disp_kernel,
        grid_spec=pltpu.PrefetchScalarGridSpec(
            num_scalar_prefetch=3,
            grid=(bsz, per_b),
            in_specs=[pl.BlockSpec((1, rows, d), lambda b, t, *_: (b, t, 0)),
                      pl.BlockSpec((1, 1, mod3.shape[2]), lambda b, t, *_: (b, 0, 0)),
                      pl.BlockSpec((1, d), lambda b, t, *_: (0, 0))],
            out_specs=pl.BlockSpec(memory_space=pl.ANY),
            scratch_shapes=[pltpu.VMEM((2, rows, d), jnp.float32),
                            pltpu.VMEM((MOE_ROWS, d), jnp.float32),
                            pltpu.SemaphoreType.DMA((2,)),
                            pltpu.SemaphoreType.DMA],
        ),
        out_shape=jax.ShapeDtypeStruct((n_slots, 1, d), jnp.float32),
        compiler_params=_params(2),
        name="dispatch",
    )(pk, pstarts, meta, x1, mod3, g_ffn)


def _moe_kernel(block_e_ref, first_ref, wslot_ref, next_e_ref, n_used_ref, xs_ref, w1_ref, w3_ref, w2_ref, yb_ref,
                xbuf, ybuf, w1buf, w3buf, w2buf, sem_in, sem_out, sem_w):
    i = pl.program_id(0)
    n_blocks = pl.num_programs(0)
    n_used = n_used_ref[0]
    rows = xbuf.shape[1]
    slot = i % 2
    ws = wslot_ref[i]

    def rows_of(ref, blk):
        return ref.at[pl.ds(pl.multiple_of(blk * rows, rows), rows), 0]

    def in_copy(blk, s):
        return pltpu.make_async_copy(rows_of(xs_ref, blk), xbuf.at[s], sem_in.at[s])

    def out_copy(blk, s):
        return pltpu.make_async_copy(ybuf.at[s], rows_of(yb_ref, blk), sem_out.at[s])

    def weight_copies(e, s):
        return [pltpu.make_async_copy(src.at[e], dst.at[s], sem_w.at[s, k])
                for k, (src, dst) in enumerate(((w1_ref, w1buf), (w3_ref, w3buf), (w2_ref, w2buf)))]

    @pl.when(i == 0)
    def _():
        in_copy(0, 0).start()
        for c in weight_copies(block_e_ref[0], 0):
            c.start()

    @pl.when(i + 1 < n_used)
    def _():
        in_copy(i + 1, 1 - slot).start()

    @pl.when(i >= 2)
    def _():
        out_copy(i - 2, slot).wait()

    @pl.when(first_ref[i] == 1)
    def _():
        for c in weight_copies(block_e_ref[i], ws):
            c.wait()

        @pl.when(next_e_ref[i] >= 0)
        def _():
            for c in weight_copies(next_e_ref[i], 1 - ws):
                c.start()

    @pl.when(i < n_used)
    def _():
        in_copy(i, slot).wait()
        x = xbuf[slot].astype(jnp.bfloat16)
        h1 = _dot(x, _unpack_rows(w1buf[ws]))
        h3 = _dot(x, _unpack_rows(w3buf[ws]))
        hid = (jax.nn.silu(h1) * h3).astype(jnp.bfloat16)
        ybuf[slot] = _dot(hid, w2buf[ws].astype(jnp.bfloat16))

    @pl.when(i >= n_used)
    def _():
        ybuf[slot] = jnp.zeros(ybuf.shape[1:], ybuf.dtype)

    out_copy(i, slot).start()

    @pl.when(i == n_blocks - 1)
    def _():
        out_copy(i, slot).wait()

        @pl.when(i >= 1)
        def _():
            out_copy(i - 1, 1 - slot).wait()


def _moe(block_e, first, wslot, next_e, n_used, xs, w1p, w3p, w2):
    n_slots, _, d = xs.shape
    f = w1p.shape[2]
    rows = MOE_ROWS
    n_blocks = n_slots // rows
    any_spec = pl.BlockSpec(memory_space=pl.ANY)
    return pl.pallas_call(
        _moe_kernel,
        grid_spec=pltpu.PrefetchScalarGridSpec(
            num_scalar_prefetch=5,
            grid=(n_blocks,),
            in_specs=[any_spec, any_spec, any_spec, any_spec],
            out_specs=any_spec,
            scratch_shapes=[pltpu.VMEM((2, rows, d), jnp.float32),
                            pltpu.VMEM((2, rows, d), jnp.float32),
                            pltpu.VMEM((2, d // 2, f), jnp.uint32),
                            pltpu.VMEM((2, d // 2, f), jnp.uint32),
                            pltpu.VMEM((2, f, d), jnp.float32),
                            pltpu.SemaphoreType.DMA((2,)),
                            pltpu.SemaphoreType.DMA((2,)),
                            pltpu.SemaphoreType.DMA((2, 3))],
        ),
        out_shape=jax.ShapeDtypeStruct((n_slots, 1, d), jnp.float32),
        compiler_params=_params(1),
        name="moe",
    )(block_e, first, wslot, next_e, n_used, xs, w1p, w3p, w2)


def _final_kernel(pk_ref, ps_ref, x1_ref, mod_ref, gr_ref, g_fin_ref, yb_ref, o_ref, ybuf, sem):
    b = pl.program_id(0)
    t = pl.program_id(1)
    per_b = pl.num_programs(1)
    g = b * per_b + t
    total = pl.num_programs(0) * per_b
    rows = x1_ref.shape[1]
    d = x1_ref.shape[2]
    seq = rows * per_b

    def issue(step, slot):
        bb = step // per_b
        tt = step - bb * per_b

        def body(j, _):
            for k in range(TOP_K):
                src = _slot_of(pk_ref, ps_ref, _assignment_index(bb, tt, rows, seq, j, k))
                _row_copy(yb_ref.at[src], ybuf.at[slot, k, pl.ds(j, 1)],
                          sem.at[slot]).start(priority=k)
            return 0

        lax.fori_loop(0, rows, body, 0, unroll=8)

    @pl.when(g == 0)
    def _():
        issue(g, 0)

    slot = g % 2

    @pl.when(g + 1 < total)
    def _():
        issue(g + 1, 1 - slot)

    def drain(j, _):
        for k in range(TOP_K):
            _row_copy(yb_ref.at[0], ybuf.at[slot, k, pl.ds(0, 1)], sem.at[slot]).wait()
        return 0

    lax.fori_loop(0, rows, drain, 0, unroll=8)

    gt2 = mod_ref[0][:, 5 * d:6 * d]
    gates = gr_ref[0]
    ffn = ybuf[slot, 0] * gates[:, 0:1] + ybuf[slot, 1] * gates[:, 1:2]
    xo = x1_ref[0] + gt2 * ffn
    ms = jnp.mean(xo * xo, axis=-1, keepdims=True)
    o_ref[0] = (xo * lax.rsqrt(ms + EPS) * g_fin_ref[...]).astype(o_ref.dtype)


def _final(pk, pstarts, x1, mod3, gate_rows, g_final, yb, out_dtype):
    bsz, seq, d = x1.shape
    rows = FIN_ROWS
    assert seq % rows == 0
    per_b = seq // rows
    tile = lambda b, t, *_: (b, t, 0)
    return pl.pallas_call(
        _final_kernel,
        grid_spec=pltpu.PrefetchScalarGridSpec(
            num_scalar_prefetch=2,
            grid=(bsz, per_b),
            in_specs=[pl.BlockSpec((1, rows, d), tile),
                      pl.BlockSpec((1, 1, mod3.shape[2]), lambda b, t, *_: (b, 0, 0)),
                      pl.BlockSpec((1, rows, LANES), tile),
                      pl.BlockSpec((1, d), lambda b, t, *_: (0, 0)),
                      pl.BlockSpec(memory_space=pl.ANY)],
            out_specs=pl.BlockSpec((1, rows, d), tile),
            scratch_shapes=[pltpu.VMEM((2, TOP_K, rows, d), jnp.float32),
                            pltpu.SemaphoreType.DMA((2,))],
        ),
        out_shape=jax.ShapeDtypeStruct((bsz, seq, d), out_dtype),
        compiler_params=_params(2),
        name="final",
    )(pk, pstarts, x1, mod3, gate_rows, g_final, yb)


def kernel(x, c, w_ada, b_ada, g_mix, w_in, lru_conv_w, lru_conv_b, lru_w_a, lru_b_a, lru_w_x, lru_b_x,
           lru_lambda, lru_out_g, sc_conv_w, sc_out_g, w_out, g_ffn, w_router_group, b_router_group,
           w_router_expert, b_router_expert, w1, w3, w2, g_final):
    out_dtype = x.dtype
    bsz, seq, d = x.shape
    depth = w_ada.shape[0]
    assert depth == 1, "the final kernel fuses the last layer's residual add with the final norm"
    n_tok = bsz * seq
    n_exp, _, f = w1.shape[1:]
    assert n_exp == N_EXPERTS and n_tok * TOP_K <= RANK_MASK
    row = lambda v: v.reshape(1, -1)
    l = 0

    c_pad = jnp.pad(c.astype(jnp.float32), ((0, 2 * SUBLANES - bsz), (0, 0)))
    n_blocks = -(-n_tok * TOP_K // MOE_ROWS) + n_exp
    n_slots = n_blocks * MOE_ROWS

    mod = _ada(c_pad, w_ada[l], row(b_ada[l]))[:bsz]
    mod3 = mod.reshape(bsz, 1, 6 * d)

    wr = jnp.concatenate([w_router_group[l], jnp.zeros((d, SUBLANES - N_GROUPS), jnp.float32),
                          w_router_expert[l], jnp.zeros((d, LANES - ROUTE_ROWS), jnp.float32)], axis=1)
    wr_hi = wr.astype(jnp.bfloat16)
    wr = jnp.concatenate([wr_hi, (wr - wr_hi.astype(jnp.float32)).astype(jnp.bfloat16)], axis=1)
    br = jnp.concatenate([b_router_group[l], jnp.zeros((SUBLANES - N_GROUPS,), jnp.float32),
                          b_router_expert[l], jnp.zeros((LANES - ROUTE_ROWS,), jnp.float32)]).reshape(LANES, 1)
    hd = lru_w_a.shape[-1]
    pack_heads = lambda w: _pack_weight(w.reshape(LRU_HEADS * hd, hd)).reshape(LRU_HEADS, hd // 2, hd)
    x1, pk, gate_rows, cnt, w1p, w3p = _mixer(
        x.astype(jnp.float32), mod3, w1[l].reshape(n_exp * d, f), w3[l].reshape(n_exp * d, f),
        row(g_mix[l]), _pack_weight(w_in[l]),
        lru_conv_w[l], row(lru_conv_b[l]), pack_heads(lru_w_a[l]), row(lru_b_a[l]),
        pack_heads(lru_w_x[l]), row(lru_b_x[l]), row(lru_lambda[l]), row(lru_out_g[l]), sc_conv_w[l],
        row(sc_out_g[l]), _pack_weight(w_out[l]), row(g_ffn[l]), wr, br)

    counts = cnt[:, 0].astype(jnp.int32)
    padded = ((counts + MOE_ROWS - 1) // MOE_ROWS) * MOE_ROWS
    pends = jnp.cumsum(padded)
    pstarts = pends - padded
    n_used = pends[-1] // MOE_ROWS
    blk_start = jnp.minimum(jnp.arange(n_blocks, dtype=jnp.int32), n_used - 1) * MOE_ROWS
    block_e = jnp.minimum(jnp.sum((blk_start[:, None] >= pends[None, :]).astype(jnp.int32), axis=1),
                          n_exp - 1)
    meta = jnp.concatenate([pstarts + counts, padded - counts, n_used[None]]).astype(jnp.int32)
    first = jnp.concatenate([jnp.ones((1,), jnp.int32), (block_e[1:] != block_e[:-1]).astype(jnp.int32)])
    wslot = (jnp.cumsum(first) - 1) % 2
    ids = jnp.arange(n_exp, dtype=jnp.int32)
    later = jnp.where((ids[None, :] > ids[:, None]) & (counts[None, :] > 0), ids[None, :], n_exp)
    next_e = jnp.min(later, axis=1)
    next_e = jnp.where(next_e < n_exp, next_e, -1)[block_e]
    pk_flat = pk.reshape(n_tok * TOP_K)

    xs = _dispatch(pk_flat, pstarts, meta, x1, mod3, row(g_ffn[l]), n_slots)
    yb = _moe(block_e, first, wslot.astype(jnp.int32), next_e.astype(jnp.int32), n_used.reshape(1), xs,
              w1p.reshape(n_exp, d // 2, f),
              w3p.reshape(n_exp, d // 2, f), w2[l])
    return _final(pk_flat, pstarts, x1, mod3, gate_rows, row(g_final), yb, out_dtype)
```

```python
import jax
import jax.numpy as jnp
from jax import lax
from jax.experimental import pallas as pl
from jax.experimental.pallas import tpu as pltpu

EPS = 1e-6
LRU_C = 8.0
LRU_HEADS = 4
SC_GROUPS = 8
N_GROUPS = 4
EXPERTS_PER_GROUP = 8
N_EXPERTS = N_GROUPS * EXPERTS_PER_GROUP
TOP_K = 2

SUBLANES = 8
LANES = 128
VMEM_LIMIT_BYTES = 60 * 1024 * 1024

ADA_BLOCK_COLS = 1536
PACK_BLOCK_ROWS = 512
MIX_ROWS = 256
DISP_ROWS = 512
MOE_ROWS = 256
FIN_ROWS = 512
ROUTE_ROWS = SUBLANES + N_EXPERTS
RANK_BITS = 20
RANK_MASK = (1 << RANK_BITS) - 1


def _const_spec(shape):
    nd = len(shape)
    return pl.BlockSpec(shape, lambda *_: (0,) * nd, pipeline_mode=pl.Buffered(1))


def _dot(a, b):
    return jnp.dot(a, b, preferred_element_type=jnp.float32)


def _pack_rows(w):
    return pltpu.bitcast(w.astype(jnp.bfloat16), jnp.uint32)


def _unpack_rows(p):
    return pltpu.bitcast(p, jnp.bfloat16)


def _params(n_axes):
    return pltpu.CompilerParams(dimension_semantics=("arbitrary",) * n_axes,
                                vmem_limit_bytes=VMEM_LIMIT_BYTES)


def _pack_kernel(w_ref, o_ref):
    o_ref[...] = _pack_rows(w_ref[...])


def _pack_weight(w2d):
    r, c = w2d.shape
    br = min(PACK_BLOCK_ROWS, r)
    assert r % br == 0
    return pl.pallas_call(
        _pack_kernel,
        grid=(r // br,),
        in_specs=[pl.BlockSpec((br, c), lambda i: (i, 0))],
        out_specs=pl.BlockSpec((br // 2, c), lambda i: (i, 0)),
        out_shape=jax.ShapeDtypeStruct((r // 2, c), jnp.uint32),
        compiler_params=_params(1),
        name="pack",
    )(w2d)


def _ada_kernel(c_ref, w_ref, b_ref, o_ref):
    cond = jax.nn.silu(c_ref[...]).astype(jnp.bfloat16)
    o_ref[...] = _dot(cond, w_ref[...].astype(jnp.bfloat16)) + b_ref[...]


def _ada(c_pad, w_ada, b_ada):
    rows, d = c_pad.shape
    cols = w_ada.shape[1]
    assert cols % ADA_BLOCK_COLS == 0
    return pl.pallas_call(
        _ada_kernel,
        grid=(cols // ADA_BLOCK_COLS,),
        in_specs=[
            pl.BlockSpec((rows, d), lambda j: (0, 0)),
            pl.BlockSpec((d, ADA_BLOCK_COLS), lambda j: (0, j)),
            pl.BlockSpec((1, ADA_BLOCK_COLS), lambda j: (0, j)),
        ],
        out_specs=pl.BlockSpec((rows, ADA_BLOCK_COLS), lambda j: (0, j)),
        out_shape=jax.ShapeDtypeStruct((rows, cols), jnp.float32),
        compiler_params=_params(1),
        name="ada",
    )(c_pad, w_ada, b_ada)


def _shift_rows(x, halo, s):
    if s == 0:
        return x
    rolled = pltpu.roll(x, s, 0)
    row = lax.broadcasted_iota(jnp.int32, (SUBLANES, x.shape[1]), 0)
    top = jnp.where(row < s, pltpu.roll(halo, s, 0), rolled[0:SUBLANES])
    return jnp.concatenate([top, rolled[SUBLANES:]], axis=0)


def _causal_conv(x, halo, w_ref, width):
    y = _shift_rows(x, halo, width - 1) * w_ref[0:1, :]
    for k in range(1, width):
        y = y + _shift_rows(x, halo, width - 1 - k) * w_ref[k:k + 1, :]
    return y


def _group_rms(y, g_ref, n_groups):
    gw = y.shape[1] // n_groups
    outs = []
    for g in range(n_groups):
        yg = y[:, g * gw:(g + 1) * gw]
        ms = jnp.mean(yg * yg, axis=-1, keepdims=True)
        outs.append(yg * lax.rsqrt(ms + EPS))
    return jnp.concatenate(outs, axis=-1) * g_ref[...]


def _linear_scan(a, u, h0):
    t, ch = a.shape
    r8 = lax.broadcasted_iota(jnp.int32, (1, SUBLANES, ch), 1)
    for d in (1, 2, 4):
        keep = r8 >= d
        a_sh = pltpu.roll(a, d, 0)
        u_sh = pltpu.roll(u, d, 0)
        u_new = (a * u_sh + u).reshape(t // SUBLANES, SUBLANES, ch)
        a_new = (a * a_sh).reshape(t // SUBLANES, SUBLANES, ch)
        u = jnp.where(keep, u_new, u.reshape(t // SUBLANES, SUBLANES, ch)).reshape(t, ch)
        a = jnp.where(keep, a_new, a.reshape(t // SUBLANES, SUBLANES, ch)).reshape(t, ch)
    carry = h0
    outs = []
    for j in range(t // SUBLANES):
        hj = u[j * SUBLANES:(j + 1) * SUBLANES] + a[j * SUBLANES:(j + 1) * SUBLANES] * carry
        outs.append(hj)
        carry = hj[SUBLANES - 1:SUBLANES]
    return jnp.concatenate(outs, axis=0), carry


def _first_argmax(v, vmax, n):
    idx = lax.broadcasted_iota(jnp.int32, v.shape, 0)
    return jnp.min(jnp.where(v == vmax, idx, n), axis=0, keepdims=True)


def _mixer_kernel(x_ref, mod_ref, w1raw_ref, w3raw_ref, g_mix_ref, w_in_ref, lcw_ref, lcb_ref, wa_ref, ba_ref, wx_ref,
                  bx_ref, lam_ref, log_ref, scw_ref, sog_ref, w_out_ref, g_ffn_ref, wr_ref, br_ref, tri_ref,
                  x1_ref, pk_ref, gr_ref, cnt_ref, w1p_ref, w3p_ref,
                  lru_halo, sc_halo, h_state, lru_halo_in, sc_halo_in, h_state_in, lt_keep, base):
    b = pl.program_id(0)
    t = pl.program_id(1)
    last = pl.num_programs(1) - 1
    d = x_ref.shape[2]
    rows = x_ref.shape[1]
    lw = lam_ref.shape[1]
    sw = scw_ref.shape[1]
    hd = lw // LRU_HEADS
    cols = (0, lw, 2 * lw, 2 * lw + sw, 2 * lw + 2 * sw, 2 * lw + 3 * sw)

    w1p_ref[...] = _pack_rows(w1raw_ref[...])
    w3p_ref[...] = _pack_rows(w3raw_ref[...])

    @pl.when(t == 0)
    def _():
        lru_halo[...] = jnp.zeros_like(lru_halo)
        sc_halo[...] = jnp.zeros_like(sc_halo)
        h_state[...] = jnp.zeros_like(h_state)
        lru_halo_in[...] = jnp.zeros_like(lru_halo_in)
        sc_halo_in[...] = jnp.zeros_like(sc_halo_in)
        h_state_in[...] = jnp.zeros_like(h_state_in)

    @pl.when((t == 0) & (b == 0))
    def _():
        base[...] = jnp.zeros_like(base)
        lt_keep[...] = jnp.zeros_like(lt_keep)

    mod = mod_ref[0]
    sh1, sc1, gt1 = mod[:, 0:d], mod[:, d:2 * d], mod[:, 2 * d:3 * d]
    sh2, sc2 = mod[:, 3 * d:4 * d], mod[:, 4 * d:5 * d]

    live = (t >= 1).astype(jnp.float32)
    lt = jnp.transpose(lt_keep[...]) + br_ref[...]
    gl = lt[0:N_GROUPS]
    gmax = jnp.max(gl, axis=0, keepdims=True)
    gsum = jnp.sum(jnp.exp(gl - gmax), axis=0, keepdims=True)
    g_p = 1.0 / gsum
    g_idx = _first_argmax(gl, gmax, N_GROUPS)
    el = lt[SUBLANES:SUBLANES + N_EXPERTS]
    sel = jnp.zeros((EXPERTS_PER_GROUP, rows), jnp.float32)
    for g in range(N_GROUPS):
        sel = sel + jnp.where(g_idx == g, el[g * EXPERTS_PER_GROUP:(g + 1) * EXPERTS_PER_GROUP], 0.0)
    emax = jnp.max(sel, axis=0, keepdims=True)
    ee = jnp.exp(sel - emax)
    ep = ee / jnp.sum(ee, axis=0, keepdims=True)
    p1 = jnp.max(ep, axis=0, keepdims=True)
    i1 = _first_argmax(ep, p1, EXPERTS_PER_GROUP)
    eidx = lax.broadcasted_iota(jnp.int32, ep.shape, 0)
    rest = jnp.where(eidx == i1, -1.0, ep)
    p2 = jnp.max(rest, axis=0, keepdims=True)
    i2 = _first_argmax(rest, p2, EXPERTS_PER_GROUP)
    psum = p1 + p2
    e1 = g_idx * EXPERTS_PER_GROUP + i1
    e2 = g_idx * EXPERTS_PER_GROUP + i2

    xidx = lax.broadcasted_iota(jnp.int32, (N_EXPERTS, rows), 0)
    oh1 = xidx == e1
    oh2 = xidx == e2
    cnt = jnp.where(oh1 | oh2, live, 0.0)
    g_lanes = jnp.concatenate([g_p * (p1 / psum), g_p * (p2 / psum),
                               jnp.zeros((LANES - TOP_K, rows), jnp.float32)], axis=0)
    gr_ref[0] = jnp.transpose(g_lanes)

    redo = t == last
    halo_l = jnp.where(redo, lru_halo_in[...], lru_halo[...])
    halo_s = jnp.where(redo, sc_halo_in[...], sc_halo[...])
    h_in = jnp.where(redo, h_state_in[...], h_state[...])
    lru_halo_in[...] = halo_l
    sc_halo_in[...] = halo_s
    h_state_in[...] = h_in

    xt = x_ref[0]
    ms = jnp.mean(xt * xt, axis=-1, keepdims=True)
    h = xt * lax.rsqrt(ms + EPS) * g_mix_ref[...]
    h = (h * (1.0 + sc1) + sh1).astype(jnp.bfloat16)

    def in_proj(i):
        return _dot(h, _unpack_rows(w_in_ref[:, cols[i]:cols[i + 1]]))

    x_lru = in_proj(0)
    xc = _causal_conv(x_lru, halo_l, lcw_ref, lcw_ref.shape[0]) + lcb_ref[...]
    lru_halo[...] = x_lru[rows - SUBLANES:rows]
    xcb = xc.astype(jnp.bfloat16)
    ra = jnp.concatenate([_dot(xcb[:, i * hd:(i + 1) * hd], _unpack_rows(wa_ref[i]))
                          for i in range(LRU_HEADS)], axis=-1)
    rx = jnp.concatenate([_dot(xcb[:, i * hd:(i + 1) * hd], _unpack_rows(wx_ref[i]))
                          for i in range(LRU_HEADS)], axis=-1)
    gate = in_proj(1)
    r = jax.nn.sigmoid(ra + ba_ref[...])
    ig = jax.nn.sigmoid(rx + bx_ref[...])
    nl = -lam_ref[...]
    softplus = jnp.maximum(nl, 0.0) + jnp.log1p(jnp.exp(-jnp.abs(nl)))
    log_a = (-LRU_C) * r * softplus
    a = jnp.exp(log_a)
    th = jnp.tanh(log_a)
    mult = jnp.sqrt(-2.0 * th / (1.0 - th))
    c_sc = in_proj(3)
    x_sc = in_proj(4)
    hl, carry = _linear_scan(a, mult * (ig * xc), h_in)
    h_state[...] = carry
    b_sc = in_proj(2)
    y_lru = _group_rms(hl * jax.nn.gelu(gate), log_ref, LRU_HEADS)
    mix_lru = _dot(y_lru.astype(jnp.bfloat16), _unpack_rows(w_out_ref[0:lw // 2, :]))

    cx = c_sc * x_sc
    y_sc = b_sc * _causal_conv(cx, halo_s, scw_ref, scw_ref.shape[0])
    sc_halo[...] = cx[rows - SUBLANES:rows]
    y_sc = _group_rms(y_sc, sog_ref, SC_GROUPS)

    mix = mix_lru + _dot(y_sc.astype(jnp.bfloat16), _unpack_rows(w_out_ref[lw // 2:(lw + sw) // 2, :]))
    x1 = xt + gt1 * mix
    x1_ref[0] = x1

    ms2 = jnp.mean(x1 * x1, axis=-1, keepdims=True)
    h2 = x1 * lax.rsqrt(ms2 + EPS) * g_ffn_ref[...]
    h2 = h2 * (1.0 + sc2) + sh2
    h2_hi = h2.astype(jnp.bfloat16)
    h2_lo = (h2 - h2_hi.astype(jnp.float32)).astype(jnp.bfloat16)
    hh = _dot(h2_hi, wr_ref[...])
    lt_keep[...] = (hh[:, 0:LANES] + _dot(h2_lo, wr_ref[:, 0:LANES])) + hh[:, LANES:2 * LANES]

    before = _dot(cnt.astype(jnp.bfloat16), tri_ref[...]) + base[:, 0:1]
    rank1 = jnp.sum(jnp.where(oh1, before, 0.0), axis=0, keepdims=True).astype(jnp.int32)
    rank2 = jnp.sum(jnp.where(oh2, before, 0.0), axis=0, keepdims=True).astype(jnp.int32)
    base[...] = base[...] + jnp.sum(cnt, axis=1, keepdims=True)
    cnt_ref[...] = base[...]
    pk_ref[0] = jnp.concatenate([(e1 << RANK_BITS) | rank1, (e2 << RANK_BITS) | rank2], axis=0)


def _mixer(x, mod3, w1raw, w3raw, g_mix, w_in, lcw, lcb, wa, ba, wx, bx, lam, log, scw, sog, w_out, g_ffn, wr, br):
    bsz, seq, d = x.shape
    rows = MIX_ROWS
    assert seq % rows == 0
    per_b = seq // rows
    steps = bsz * per_b
    lw = lam.shape[1]
    sw = scw.shape[1]
    raws = (w1raw, w3raw)
    assert all(w.shape[0] % (2 * SUBLANES * steps) == 0 for w in raws)
    tri = (lax.broadcasted_iota(jnp.int32, (rows, rows), 0)
           < lax.broadcasted_iota(jnp.int32, (rows, rows), 1)).astype(jnp.bfloat16)
    consts = [g_mix, w_in, lcw, lcb, wa, ba, wx, bx, lam, log, scw, sog, w_out, g_ffn, wr, br, tri]
    tile = lambda bb, tt: (bb, jnp.minimum(tt, per_b - 1), 0)
    routed = lambda bb, tt: (bb, jnp.maximum(tt - 1, 0), 0)
    routed_lanes = lambda bb, tt: (bb, 0, jnp.maximum(tt - 1, 0))
    step = lambda bb, tt: (bb * per_b + jnp.minimum(tt, per_b - 1), 0)
    return pl.pallas_call(
        _mixer_kernel,
        grid=(bsz, per_b + 1),
        in_specs=[pl.BlockSpec((1, rows, d), tile),
                  pl.BlockSpec((1, 1, mod3.shape[2]), lambda bb, tt: (bb, 0, 0))]
                 + [pl.BlockSpec((w.shape[0] // steps, w.shape[1]), step) for w in raws]
                 + [_const_spec(c.shape) for c in consts],
        out_specs=[pl.BlockSpec((1, rows, d), tile),
                   pl.BlockSpec((1, TOP_K, rows), routed_lanes),
                   pl.BlockSpec((1, rows, LANES), routed),
                   pl.BlockSpec((N_EXPERTS, LANES), lambda bb, tt: (0, 0))]
                  + [pl.BlockSpec((w.shape[0] // steps // 2, w.shape[1]), step) for w in raws],
        out_shape=[jax.ShapeDtypeStruct((bsz, seq, d), jnp.float32),
                   jax.ShapeDtypeStruct((bsz, TOP_K, seq), jnp.int32),
                   jax.ShapeDtypeStruct((bsz, seq, LANES), jnp.float32),
                   jax.ShapeDtypeStruct((N_EXPERTS, LANES), jnp.float32)]
                  + [jax.ShapeDtypeStruct((w.shape[0] // 2, w.shape[1]), jnp.uint32) for w in raws],
        scratch_shapes=[pltpu.VMEM((SUBLANES, lw), jnp.float32),
                        pltpu.VMEM((SUBLANES, sw), jnp.float32),
                        pltpu.VMEM((1, lw), jnp.float32)] * 2
                       + [pltpu.VMEM((rows, LANES), jnp.float32),
                          pltpu.VMEM((N_EXPERTS, LANES), jnp.float32)],
        compiler_params=_params(2),
        name="mixer",
    )(x, mod3, *raws, *consts)


def _row_copy(src, dst, sem):
    return pltpu.make_async_copy(src, dst, sem)


def _slot_of(pk_ref, ps_ref, idx):
    p = pk_ref[idx]
    return ps_ref[lax.shift_right_logical(p, RANK_BITS)] + (p & RANK_MASK)


def _assignment_index(b, t, rows, seq, j, k):
    return (b * TOP_K + k) * seq + t * rows + j


def _disp_kernel(pk_ref, ps_ref, meta_ref, x1_ref, mod_ref, g_ffn_ref, xs_ref, h2buf, zbuf, sem, sem_pad):
    b = pl.program_id(0)
    t = pl.program_id(1)
    i = b * pl.num_programs(1) + t
    total = pl.num_programs(0) * pl.num_programs(1)
    rows = x1_ref.shape[1]
    d = x1_ref.shape[2]
    seq = rows * pl.num_programs(1)
    block_rows = zbuf.shape[0]
    n_blocks = xs_ref.shape[0] // block_rows
    slot = i % 2

    @pl.when(i == 0)
    def _():
        zbuf[...] = jnp.zeros_like(zbuf)

    mod = mod_ref[0]
    sh2, sc2 = mod[:, 3 * d:4 * d], mod[:, 4 * d:5 * d]
    x1 = x1_ref[0]
    ms2 = jnp.mean(x1 * x1, axis=-1, keepdims=True)
    h2 = x1 * lax.rsqrt(ms2 + EPS) * g_ffn_ref[...]
    h2buf[slot] = h2 * (1.0 + sc2) + sh2

    pad_start = meta_ref[jnp.minimum(i, N_EXPERTS - 1)]
    pad_len = meta_ref[N_EXPERTS + jnp.minimum(i, N_EXPERTS - 1)]
    n_used = meta_ref[2 * N_EXPERTS]

    def pad_copies(fn):
        @pl.when(i < N_EXPERTS)
        def _():
            def body(j, _):
                fn(_row_copy(zbuf.at[pl.ds(0, 1)], xs_ref.at[pad_start + j], sem_pad))
                return 0
            lax.fori_loop(0, pad_len, body, 0)

    pad_copies(lambda c: c.start())

    def tail_copies(fn):
        @pl.when(i == N_EXPERTS)
        def _():
            def body(blk, _):
                dst = pl.multiple_of(blk * block_rows, block_rows)
                fn(_row_copy(zbuf, xs_ref.at[pl.ds(dst, block_rows), 0], sem_pad))
                return 0
            lax.fori_loop(n_used, n_blocks, body, 0)

    tail_copies(lambda c: c.start())

    def issue(j, _):
        for k in range(TOP_K):
            dst = _slot_of(pk_ref, ps_ref, _assignment_index(b, t, rows, seq, j, k))
            _row_copy(h2buf.at[slot, pl.ds(j, 1)], xs_ref.at[dst], sem.at[slot]).start(priority=k)
        return 0

    lax.fori_loop(0, rows, issue, 0, unroll=8)

    def drain(s):
        def body(j, _):
            for k in range(TOP_K):
                _row_copy(h2buf.at[s, pl.ds(0, 1)], xs_ref.at[0], sem.at[s]).wait()
            return 0
        lax.fori_loop(0, rows, body, 0, unroll=8)

    @pl.when(i >= 1)
    def _():
        drain(1 - slot)

    @pl.when(i == total - 1)
    def _():
        drain(slot)

    pad_copies(lambda c: c.wait())
    tail_copies(lambda c: c.wait())


def _dispatch(pk, pstarts, meta, x1, mod3, g_ffn, n_slots):
    bsz, seq, d = x1.shape
    rows = DISP_ROWS
    assert seq % rows == 0
    per_b = seq // rows
    assert bsz * per_b > N_EXPERTS
    return pl.pallas_call(
        _disp_kernel,
        grid_spec=pltpu.PrefetchScalarGridSpec(
            num_scalar_prefetch=3,
            grid=(bsz, per_b),
            in_specs=[pl.BlockSpec((1, rows, d), lambda b, t, *_: (b, t, 0)),
                      pl.BlockSpec((1, 1, mod3.shape[2]), lambda b, t, *_: (b, 0, 0)),
                      pl.BlockSpec((1, d), lambda b, t, *_: (0, 0))],
            out_specs=pl.BlockSpec(memory_space=pl.ANY),
            scratch_shapes=[pltpu.VMEM((2, rows, d), jnp.float32),
                            pltpu.VMEM((MOE_ROWS, d), jnp.float32),
                            pltpu.SemaphoreType.DMA((2,)),
                            pltpu.SemaphoreType.DMA],
        ),
        out_shape=jax.ShapeDtypeStruct((n_slots, 1, d), jnp.float32),
        compiler_params=_params(2),
        name="dispatch",
    )(pk, pstarts, meta, x1, mod3, g_ffn)


def _moe_kernel(block_e_ref, first_ref, wslot_ref, next_e_ref, n_used_ref, xs_ref, w1_ref, w3_ref, w2_ref, yb_ref,
                xbuf, ybuf, w1buf, w3buf, w2buf, sem_in, sem_out, sem_w):
    i = pl.program_id(0)
    n_blocks = pl.num_programs(0)
    n_used = n_used_ref[0]
    rows = xbuf.shape[1]
    slot = i % 2
    ws = wslot_ref[i]

    def rows_of(ref, blk):
        return ref.at[pl.ds(pl.multiple_of(blk * rows, rows), rows), 0]

    def in_copy(blk, s):
        return pltpu.make_async_copy(rows_of(xs_ref, blk), xbuf.at[s], sem_in.at[s])

    def out_copy(blk, s):
        return pltpu.make_async_copy(ybuf.at[s], rows_of(yb_ref, blk), sem_out.at[s])

    def weight_copies(e, s):
        return [pltpu.make_async_copy(src.at[e], dst.at[s], sem_w.at[s, k])
                for k, (src, dst) in enumerate(((w1_ref, w1buf), (w3_ref, w3buf), (w2_ref, w2buf)))]

    @pl.when(i == 0)
    def _():
        in_copy(0, 0).start()
        for c in weight_copies(block_e_ref[0], 0):
            c.start()

    @pl.when(i + 1 < n_used)
    def _():
        in_copy(i + 1, 1 - slot).start()

    @pl.when(i >= 2)
    def _():
        out_copy(i - 2, slot).wait()

    @pl.when(first_ref[i] == 1)
    def _():
        for c in weight_copies(block_e_ref[i], ws):
            c.wait()

        @pl.when(next_e_ref[i] >= 0)
        def _():
            for c in weight_copies(next_e_ref[i], 1 - ws):
                c.start()

    @pl.when(i < n_used)
    def _():
        in_copy(i, slot).wait()
        x = xbuf[slot].astype(jnp.bfloat16)
        h1 = _dot(x, _unpack_rows(w1buf[ws]))
        h3 = _dot(x, _unpack_rows(w3buf[ws]))
        hid = (jax.nn.silu(h1) * h3).astype(jnp.bfloat16)
        ybuf[slot] = _dot(hid, w2buf[ws].astype(jnp.bfloat16))

    @pl.when(i >= n_used)
    def _():
        ybuf[slot] = jnp.zeros(ybuf.shape[1:], ybuf.dtype)

    out_copy(i, slot).start()

    @pl.when(i == n_blocks - 1)
    def _():
        out_copy(i, slot).wait()

        @pl.when(i >= 1)
        def _():
            out_copy(i - 1, 1 - slot).wait()


def _moe(block_e, first, wslot, next_e, n_used, xs, w1p, w3p, w2):
    n_slots, _, d = xs.shape
    f = w1p.shape[2]
    rows = MOE_ROWS
    n_blocks = n_slots // rows
    any_spec = pl.BlockSpec(memory_space=pl.ANY)
    return pl.pallas_call(
        _moe_kernel,
        grid_spec=pltpu.PrefetchScalarGridSpec(
            num_scalar_prefetch=5,
            grid=(n_blocks,),
            in_specs=[any_spec, any_spec, any_spec, any_spec],
            out_specs=any_spec,
            scratch_shapes=[pltpu.VMEM((2, rows, d), jnp.float32),
                            pltpu.VMEM((2, rows, d), jnp.float32),
                            pltpu.VMEM((2, d // 2, f), jnp.uint32),
                            pltpu.VMEM((2, d // 2, f), jnp.uint32),
                            pltpu.VMEM((2, f, d), jnp.float32),
                            pltpu.SemaphoreType.DMA((2,)),
                            pltpu.SemaphoreType.DMA((2,)),
                            pltpu.SemaphoreType.DMA((2, 3))],
        ),
        out_shape=jax.ShapeDtypeStruct((n_slots, 1, d), jnp.float32),
        compiler_params=_params(1),
        name="moe",
    )(block_e, first, wslot, next_e, n_used, xs, w1p, w3p, w2)


def _final_kernel(pk_ref, ps_ref, x1_ref, mod_ref, gr_ref, g_fin_ref, yb_ref, o_ref, ybuf, sem):
    b = pl.program_id(0)
    t = pl.program_id(1)
    per_b = pl.num_programs(1)
    g = b * per_b + t
    total = pl.num_programs(0) * per_b
    rows = x1_ref.shape[1]
    d = x1_ref.shape[2]
    seq = rows * per_b

    def issue(step, slot):
        bb = step // per_b
        tt = step - bb * per_b

        def body(j, _):
            for k in range(TOP_K):
                src = _slot_of(pk_ref, ps_ref, _assignment_index(bb, tt, rows, seq, j, k))
                _row_copy(yb_ref.at[src], ybuf.at[slot, k, pl.ds(j, 1)],
                          sem.at[slot]).start(priority=k)
            return 0

        lax.fori_loop(0, rows, body, 0, unroll=8)

    @pl.when(g == 0)
    def _():
        issue(g, 0)

    slot = g % 2

    @pl.when(g + 1 < total)
    def _():
        issue(g + 1, 1 - slot)

    def drain(j, _):
        for k in range(TOP_K):
            _row_copy(yb_ref.at[0], ybuf.at[slot, k, pl.ds(0, 1)], sem.at[slot]).wait()
        return 0

    lax.fori_loop(0, rows, drain, 0, unroll=8)

    gt2 = mod_ref[0][:, 5 * d:6 * d]
    gates = gr_ref[0]
    ffn = ybuf[slot, 0] * gates[:, 0:1] + ybuf[slot, 1] * gates[:, 1:2]
    xo = x1_ref[0] + gt2 * ffn
    ms = jnp.mean(xo * xo, axis=-1, keepdims=True)
    o_ref[0] = (xo * lax.rsqrt(ms + EPS) * g_fin_ref[...]).astype(o_ref.dtype)


def _final(pk, pstarts, x1, mod3, gate_rows, g_final, yb, out_dtype):
    bsz, seq, d = x1.shape
    rows = FIN_ROWS
    assert seq % rows == 0
    per_b = seq // rows
    tile = lambda b, t, *_: (b, t, 0)
    return pl.pallas_call(
        _final_kernel,
        grid_spec=pltpu.PrefetchScalarGridSpec(
            num_scalar_prefetch=2,
            grid=(bsz, per_b),
            in_specs=[pl.BlockSpec((1, rows, d), tile),
                      pl.BlockSpec((1, 1, mod3.shape[2]), lambda b, t, *_: (b, 0, 0)),
                      pl.BlockSpec((1, rows, LANES), tile),
                      pl.BlockSpec((1, d), lambda b, t, *_: (0, 0)),
                      pl.BlockSpec(memory_space=pl.ANY)],
            out_specs=pl.BlockSpec((1, rows, d), tile),
            scratch_shapes=[pltpu.VMEM((2, TOP_K, rows, d), jnp.float32),
                            pltpu.SemaphoreType.DMA((2,))],
        ),
        out_shape=jax.ShapeDtypeStruct((bsz, seq, d), out_dtype),
        compiler_params=_params(2),
        name="final",
    )(pk, pstarts, x1, mod3, gate_rows, g_final, yb)


def kernel(x, c, w_ada, b_ada, g_mix, w_in, lru_conv_w, lru_conv_b, lru_w_a, lru_b_a, lru_w_x, lru_b_x,
           lru_lambda, lru_out_g, sc_conv_w, sc_out_g, w_out, g_ffn, w_router_group, b_router_group,
           w_router_expert, b_router_expert, w1, w3, w2, g_final):
    out_dtype = x.dtype
    bsz, seq, d = x.shape
    depth = w_ada.shape[0]
    assert depth == 1, "the final kernel fuses the last layer's residual add with the final norm"
    n_tok = bsz * seq
    n_exp, _, f = w1.shape[1:]
    assert n_exp == N_EXPERTS and n_tok * TOP_K <= RANK_MASK
    row = lambda v: v.reshape(1, -1)
    l = 0

    c_pad = jnp.pad(c.astype(jnp.float32), ((0, 2 * SUBLANES - bsz), (0, 0)))
    n_blocks = -(-n_tok * TOP_K // MOE_ROWS) + n_exp
    n_slots = n_blocks * MOE_ROWS

    mod = _ada(c_pad, w_ada[l], row(b_ada[l]))[:bsz]
    mod3 = mod.reshape(bsz, 1, 6 * d)

    wr = jnp.concatenate([w_router_group[l], jnp.zeros((d, SUBLANES - N_GROUPS), jnp.float32),
                          w_router_expert[l], jnp.zeros((d, LANES - ROUTE_ROWS), jnp.float32)], axis=1)
    wr_hi = wr.astype(jnp.bfloat16)
    wr = jnp.concatenate([wr_hi, (wr - wr_hi.astype(jnp.float32)).astype(jnp.bfloat16)], axis=1)
    br = jnp.concatenate([b_router_group[l], jnp.zeros((SUBLANES - N_GROUPS,), jnp.float32),
                          b_router_expert[l], jnp.zeros((LANES - ROUTE_ROWS,), jnp.float32)]).reshape(LANES, 1)
    hd = lru_w_a.shape[-1]
    pack_heads = lambda w: _pack_weight(w.reshape(LRU_HEADS * hd, hd)).reshape(LRU_HEADS, hd // 2, hd)
    x1, pk, gate_rows, cnt, w1p, w3p = _mixer(
        x.astype(jnp.float32), mod3, w1[l].reshape(n_exp * d, f), w3[l].reshape(n_exp * d, f),
        row(g_mix[l]), _pack_weight(w_in[l]),
        lru_conv_w[l], row(lru_conv_b[l]), pack_heads(lru_w_a[l]), row(lru_b_a[l]),
        pack_heads(lru_w_x[l]), row(lru_b_x[l]), row(lru_lambda[l]), row(lru_out_g[l]), sc_conv_w[l],
        row(sc_out_g[l]), _pack_weight(w_out[l]), row(g_ffn[l]), wr, br)

    counts = cnt[:, 0].astype(jnp.int32)
    padded = ((counts + MOE_ROWS - 1) // MOE_ROWS) * MOE_ROWS
    pends = jnp.cumsum(padded)
    pstarts = pends - padded
    n_used = pends[-1] // MOE_ROWS
    blk_start = jnp.minimum(jnp.arange(n_blocks, dtype=jnp.int32), n_used - 1) * MOE_ROWS
    block_e = jnp.minimum(jnp.sum((blk_start[:, None] >= pends[None, :]).astype(jnp.int32), axis=1),
                          n_exp - 1)
    meta = jnp.concatenate([pstarts + counts, padded - counts, n_used[None]]).astype(jnp.int32)
    first = jnp.concatenate([jnp.ones((1,), jnp.int32), (block_e[1:] != block_e[:-1]).astype(jnp.int32)])
    wslot = (jnp.cumsum(first) - 1) % 2
    ids = jnp.arange(n_exp, dtype=jnp.int32)
    later = jnp.where((ids[None, :] > ids[:, None]) & (counts[None, :] > 0), ids[None, :], n_exp)
    next_e = jnp.min(later, axis=1)
    next_e = jnp.where(next_e < n_exp, next_e, -1)[block_e]
    pk_flat = pk.reshape(n_tok * TOP_K)

    xs = _dispatch(pk_flat, pstarts, meta, x1, mod3, row(g_ffn[l]), n_slots)
    yb = _moe(block_e, first, wslot.astype(jnp.int32), next_e.astype(jnp.int32), n_used.reshape(1), xs,
              w1p.reshape(n_exp, d // 2, f),
              w3p.reshape(n_exp, d // 2, f), w2[l])
    return _final(pk_flat, pstarts, x1, mod3, gate_rows, row(g_final), yb, out_dtype)
```
